```python
import functools
import jax, jax.numpy as jnp
from jax import lax
import numpy as np

D_MODEL = 2048
BATCH = 1
SEQ = 8192
DEPTH = 1
DEC_BATCH = 128
DEC_SEQ = 4
PAST_LEN = 2048
PAGE_SIZE = 128

N_META = 16
D_RNN = D_MODEL
N_LRU_BLOCKS = 16
LRU_BLOCK = D_RNN // N_LRU_BLOCKS
CONV_W = 4
LRU_C = 8.0
N_HEADS = 16
HEAD_DIM = D_MODEL // N_HEADS
D_ATTN = N_HEADS * HEAD_DIM
N_IDX_HEADS = 16
IDX_DIM = 64
TOPK_MAX = 256
Q_BLOCK = 128
NORM_EPS = 1e-6
NEG_INF = -1e30
COL_SIZES = (D_RNN, D_RNN, D_ATTN, D_ATTN, D_ATTN, D_ATTN,
             N_IDX_HEADS * IDX_DIM, IDX_DIM, N_IDX_HEADS, D_MODEL, D_MODEL)
D_IN_TOTAL = 2 * D_RNN + 4 * D_ATTN + N_IDX_HEADS * IDX_DIM + IDX_DIM + N_IDX_HEADS + 2 * D_MODEL

kernel_name = "hawk_dsa_gated_parallel_step"


def _rmsnorm(x, g):
    x32 = x.astype(jnp.float32)
    y = x32 * lax.rsqrt(jnp.mean(x32 * x32, axis=-1, keepdims=True) + NORM_EPS)
    return (y * g.astype(jnp.float32)).astype(x.dtype)


def _split_cols(z):
    cuts = np.cumsum(COL_SIZES)[:-1].tolist()
    return jnp.split(z, cuts, axis=-1)


def _alibi_slopes():
    return jnp.asarray(2.0 ** (-8.0 * np.arange(1, N_HEADS + 1) / N_HEADS), dtype=jnp.float32)


def _rglru_branch(xr, conv_prefix, h0, conv_w, conv_b, wa, ba, wx, bx, lam):
    B, T, _ = xr.shape
    xpad = jnp.concatenate([conv_prefix.astype(xr.dtype), xr], axis=1)
    xc = conv_b
    for j in range(CONV_W):
        xc = xc + xpad[:, j:j + T] * conv_w[j]
    xb = xc.reshape(B, T, N_LRU_BLOCKS, LRU_BLOCK)
    r = jax.nn.sigmoid(jnp.einsum('btnc,ncd->btnd', xb, wa).reshape(B, T, D_RNN) + ba)
    i = jax.nn.sigmoid(jnp.einsum('btnc,ncd->btnd', xb, wx).reshape(B, T, D_RNN) + bx)
    log_a = -LRU_C * jax.nn.softplus(-lam.astype(jnp.float32)) * r.astype(jnp.float32)
    a = jnp.exp(log_a)
    u = jnp.sqrt(-jnp.expm1(2.0 * log_a)) * (i * xc).astype(jnp.float32)

    def step(h, au):
        a_t, u_t = au
        h = a_t * h + u_t
        return h, h

    h_last, hs = lax.scan(step, h0.astype(jnp.float32), (jnp.swapaxes(a, 0, 1), jnp.swapaxes(u, 0, 1)))
    return jnp.swapaxes(hs, 0, 1).astype(xr.dtype), h_last, xpad[:, T:]


def _select_keys(qi, wi, ki, pos_q, k_sel):
    s = jnp.einsum('bqhd,bsd->bqhs', qi, ki, preferred_element_type=jnp.float32) * (IDX_DIM ** -0.5)
    score = jnp.einsum('bqhs,bqh->bqs', jax.nn.relu(s), wi.astype(jnp.float32))
    pos_k = jnp.arange(ki.shape[1])
    score = jnp.where(pos_k[None, None, :] <= pos_q[..., None], score, NEG_INF)
    _, idx = lax.top_k(score, k_sel)
    return idx, idx <= pos_q[..., None]


def _sparse_attend(q, kg, vg, pos_q, idx, valid):
    s = jnp.einsum('bqhd,bqkhd->bqhk', q, kg, preferred_element_type=jnp.float32) * (HEAD_DIM ** -0.5)
    dist = (pos_q[..., None] - idx).astype(jnp.float32)
    s = s - _alibi_slopes()[None, None, :, None] * dist[:, :, None, :]
    s = jnp.where(valid[:, :, None, :], s, NEG_INF)
    p = jax.nn.softmax(s, axis=-1)
    return jnp.einsum('bqhk,bqkhd->bqhd', p.astype(vg.dtype), vg)


def _prompt_attend(q, k, v, qi, wi, ki, k_sel):
    B, T = q.shape[:2]
    n_blk = -(-T // Q_BLOCK)
    pad = n_blk * Q_BLOCK - T
    padq = lambda a: jnp.pad(a, [(0, 0), (0, pad)] + [(0, 0)] * (a.ndim - 2))
    qp, qip, wip = padq(q), padq(qi), padq(wi)
    take = jax.vmap(lambda rows, ids: rows[ids])

    def block(start):
        qb = lax.dynamic_slice_in_dim(qp, start, Q_BLOCK, axis=1)
        qib = lax.dynamic_slice_in_dim(qip, start, Q_BLOCK, axis=1)
        wib = lax.dynamic_slice_in_dim(wip, start, Q_BLOCK, axis=1)
        pos_q = jnp.broadcast_to(start + jnp.arange(Q_BLOCK), (B, Q_BLOCK))
        idx, valid = _select_keys(qib, wib, ki, pos_q, k_sel)
        return _sparse_attend(qb, take(k, idx), take(v, idx), pos_q, idx, valid)

    out = lax.map(block, jnp.arange(n_blk) * Q_BLOCK)
    return jnp.moveaxis(out, 0, 1).reshape(B, n_blk * Q_BLOCK, N_HEADS, HEAD_DIM)[:, :T]


def _sample_attend(q, k, v, qi, wi, ki, cache_k, cache_v, cache_kidx, page_table, layer, k_sel):
    S = q.shape[1]
    past_len = page_table.shape[1] * PAGE_SIZE

    def one_seq(args):
        qb, kb, vb, qib, wib, kib, pages = args
        ki_past = cache_kidx[layer, pages].reshape(past_len, IDX_DIM).astype(kib.dtype)
        ki_all = jnp.concatenate([ki_past, kib], axis=0)
        pos_q = (past_len + jnp.arange(S))[None]
        idx, valid = _select_keys(qib[None], wib[None], ki_all[None], pos_q, k_sel)
        from_new = idx >= past_len
        pidx = jnp.minimum(idx, past_len - 1)
        phys = pages[pidx // PAGE_SIZE]
        slot = pidx % PAGE_SIZE
        nidx = jnp.clip(idx - past_len, 0, S - 1)

        def gather(pool, new):
            return jnp.where(from_new[..., None, None], new[nidx], pool[layer, phys, slot].astype(new.dtype))

        return _sparse_attend(qb[None], gather(cache_k, kb), gather(cache_v, vb), pos_q, idx, valid)[0]

    return lax.map(one_seq, (q, k, v, qi, wi, ki, page_table))


def _layer(x, conv_prefix, h0, attend, lp):
    (norm_g, w_in, conv_w, conv_b, wa, ba, wx, bx, lam, w_pa, w_pb, w_out) = lp
    B, T, _ = x.shape
    (xr, gr, q, k, v, ga, qi, ki, wi, g_a, g_b) = _split_cols(_rmsnorm(x, norm_g) @ w_in)
    y_rnn, h_last, conv_last = _rglru_branch(xr, conv_prefix, h0, conv_w, conv_b, wa, ba, wx, bx, lam)
    y_rnn = y_rnn * jax.nn.silu(gr)
    q = q.reshape(B, T, N_HEADS, HEAD_DIM)
    k = k.reshape(B, T, N_HEADS, HEAD_DIM)
    v = v.reshape(B, T, N_HEADS, HEAD_DIM)
    qi = qi.reshape(B, T, N_IDX_HEADS, IDX_DIM)
    wi = wi * (N_IDX_HEADS ** -0.5)
    o = attend(q, k, v, qi, wi, ki).reshape(B, T, D_ATTN) * jax.nn.silu(ga)
    merged = jax.nn.sigmoid(g_a) * (y_rnn @ w_pa) + jax.nn.sigmoid(g_b) * (o @ w_pb)
    return x + merged @ w_out, (k, v, ki, h_last, conv_last)


def setup_inputs(seed: int = 0) -> dict:
    key = jax.random.key(seed)
    ks = jax.random.split(key, 24)
    f32 = jnp.float32
    n_pages = PAST_LEN // PAGE_SIZE
    n_used = DEC_BATCH * n_pages
    n_phys = n_used + max(1, n_used // 4)
    nrm = lambda k, shape, s=1.0: jax.random.normal(k, shape, f32) * s
    page_table = jax.random.permutation(ks[0], n_phys)[:n_used].reshape(DEC_BATCH, n_pages).astype(jnp.int32)
    a_c = jax.random.uniform(ks[1], (DEPTH, D_RNN), f32, 0.9, 0.999)
    sig = a_c ** (1.0 / LRU_C)
    lru_lambda = jnp.log(sig) - jnp.log1p(-sig)
    return {
        "x_prompt": nrm(ks[2], (BATCH, SEQ, D_MODEL)),
        "x_sample": nrm(ks[3], (DEC_BATCH, DEC_SEQ, D_MODEL)),
        "cache_k": nrm(ks[4], (DEPTH, n_phys, PAGE_SIZE, N_HEADS, HEAD_DIM)),
        "cache_v": nrm(ks[5], (DEPTH, n_phys, PAGE_SIZE, N_HEADS, HEAD_DIM)),
        "cache_kidx": nrm(ks[6], (DEPTH, n_phys, PAGE_SIZE, IDX_DIM)),
        "state_h": nrm(ks[7], (DEPTH, DEC_BATCH, D_RNN), 0.5),
        "state_conv": nrm(ks[8], (DEPTH, DEC_BATCH, CONV_W - 1, D_RNN)),
        "page_table": page_table,
        "meta_tokens": nrm(ks[9], (N_META, D_MODEL)),
        "norm_g": 1.0 + nrm(ks[10], (DEPTH, D_MODEL), 0.01),
        "w_in": nrm(ks[11], (DEPTH, D_MODEL, D_IN_TOTAL), D_MODEL ** -0.5),
        "conv_w": nrm(ks[12], (DEPTH, CONV_W, D_RNN), CONV_W ** -0.5),
        "conv_b": nrm(ks[13], (DEPTH, D_RNN), 0.01),
        "lru_wa": nrm(ks[14], (DEPTH, N_LRU_BLOCKS, LRU_BLOCK, LRU_BLOCK), LRU_BLOCK ** -0.5),
        "lru_ba": nrm(ks[15], (DEPTH, D_RNN), 0.01),
        "lru_wx": nrm(ks[16], (DEPTH, N_LRU_BLOCKS, LRU_BLOCK, LRU_BLOCK), LRU_BLOCK ** -0.5),
        "lru_bx": nrm(ks[17], (DEPTH, D_RNN), 0.01),
        "lru_lambda": lru_lambda,
        "w_proj_a": nrm(ks[18], (DEPTH, D_RNN, D_MODEL), D_RNN ** -0.5),
        "w_proj_b": nrm(ks[19], (DEPTH, D_ATTN, D_MODEL), D_ATTN ** -0.5),
        "w_out": nrm(ks[20], (DEPTH, D_MODEL, D_MODEL), D_MODEL ** -0.5),
        "final_g": 1.0 + nrm(ks[21], (D_MODEL,), 0.01),
    }


def reference(x_prompt, x_sample, cache_k, cache_v, cache_kidx, state_h, state_conv, page_table,
              meta_tokens, norm_g, w_in, conv_w, conv_b, lru_wa, lru_ba, lru_wx, lru_bx, lru_lambda,
              w_proj_a, w_proj_b, w_out, final_g):
    B = x_prompt.shape[0]
    t_prompt = x_prompt.shape[1] + N_META
    k_sel_prompt = min(TOPK_MAX, t_prompt // 4)
    k_sel_sample = min(TOPK_MAX, (page_table.shape[1] * PAGE_SIZE + x_sample.shape[1]) // 4)
    meta = jnp.broadcast_to(meta_tokens.astype(x_prompt.dtype)[None], (B, N_META, D_MODEL))
    xp = jnp.concatenate([meta, x_prompt], axis=1)
    xs = x_sample
    prompt_attend = functools.partial(_prompt_attend, k_sel=k_sel_prompt)
    sp, ss = [], []
    for layer in range(DEPTH):
        lp = (norm_g[layer], w_in[layer], conv_w[layer], conv_b[layer], lru_wa[layer], lru_ba[layer],
              lru_wx[layer], lru_bx[layer], lru_lambda[layer], w_proj_a[layer], w_proj_b[layer], w_out[layer])
        xp, st_p = _layer(xp, jnp.zeros((B, CONV_W - 1, D_RNN), xp.dtype), jnp.zeros((B, D_RNN), jnp.float32),
                          prompt_attend, lp)
        sample_attend = functools.partial(_sample_attend, cache_k=cache_k, cache_v=cache_v,
                                          cache_kidx=cache_kidx, page_table=page_table,
                                          layer=layer, k_sel=k_sel_sample)
        xs, st_s = _layer(xs, state_conv[layer], state_h[layer], sample_attend, lp)
        sp.append(st_p)
        ss.append(st_s)
    y_prompt = _rmsnorm(xp, final_g)[:, N_META:]
    y_sample = _rmsnorm(xs, final_g)
    k_prompt = jnp.stack([s[0] for s in sp])
    v_prompt = jnp.stack([s[1] for s in sp])
    kidx_prompt = jnp.stack([s[2] for s in sp])
    h_prompt = jnp.stack([s[3] for s in sp])
    conv_prompt = jnp.stack([s[4] for s in sp])
    k_sample = jnp.stack([s[0] for s in ss])
    v_sample = jnp.stack([s[1] for s in ss])
    kidx_sample = jnp.stack([s[2] for s in ss])
    h_sample = jnp.stack([s[3] for s in ss])
    conv_sample = jnp.stack([s[4] for s in ss])
    return (y_prompt, y_sample, k_prompt, v_prompt, kidx_prompt, h_prompt, conv_prompt,
            k_sample, v_sample, kidx_sample, h_sample, conv_sample)
```

```python
import functools

import numpy as np
import jax
import jax.numpy as jnp
from jax import lax
from jax.experimental import pallas as pl
from jax.experimental.pallas import tpu as pltpu

F32 = jnp.float32
BF16 = jnp.bfloat16
I32 = jnp.int32

D_MODEL = 2048
N_HEADS = 16
HEAD_DIM = 128
N_IDX_HEADS = 16
IDX_DIM = 64
N_LRU_BLOCKS = 16
LRU_BLOCK = 128
CONV_W = 4
LRU_C = 8.0
N_META = 16
TOPK = 256
PAGE = 128
NORM_EPS = 1e-6
NEG_BIG = -1e30
INT_MIN = -(2 ** 31)

ROW_BLOCK = 256
Q_BLOCK = 256
S_CHUNK = 256
VMEM_LIMIT = 56 * 1024 * 1024


def _cparams(*sem):
    return pltpu.CompilerParams(dimension_semantics=sem, vmem_limit_bytes=VMEM_LIMIT)


def _rmsnorm_kernel(x_ref, g_ref, o_ref):
    x = x_ref[...]
    y = x * lax.rsqrt(jnp.mean(x * x, axis=-1, keepdims=True) + NORM_EPS)
    o_ref[...] = (y * g_ref[...]).astype(o_ref.dtype)


def _rmsnorm_bf16(x, g):
    m, d = x.shape
    return pl.pallas_call(
        _rmsnorm_kernel,
        grid=(m // ROW_BLOCK,),
        in_specs=[pl.BlockSpec((ROW_BLOCK, d), lambda i: (i, 0)),
                  pl.BlockSpec((1, d), lambda i: (0, 0))],
        out_specs=pl.BlockSpec((ROW_BLOCK, d), lambda i: (i, 0)),
        out_shape=jax.ShapeDtypeStruct((m, d), BF16),
        compiler_params=_cparams("parallel"),
        name="rmsnorm_in",
    )(x, g.reshape(1, d))


def _mm_kernel(x_ref, w_ref, o_ref, *, scale):
    acc = jnp.dot(x_ref[...], w_ref[...], preferred_element_type=F32)
    if scale is not None:
        acc = acc * scale
    o_ref[...] = acc.astype(o_ref.dtype)


def _matmul(x, w, out_dtype, *, scale=None, tm=1280, tn=1024, name="matmul"):
    m, k = x.shape
    n = w.shape[1]
    tn = min(tn, n)
    assert m % tm == 0 and n % tn == 0
    return pl.pallas_call(
        functools.partial(_mm_kernel, scale=scale),
        grid=(n // tn, m // tm),
        in_specs=[pl.BlockSpec((tm, k), lambda j, i: (i, 0)),
                  pl.BlockSpec((k, tn), lambda j, i: (0, j))],
        out_specs=pl.BlockSpec((tm, tn), lambda j, i: (i, j)),
        out_shape=jax.ShapeDtypeStruct((m, n), out_dtype),
        compiler_params=_cparams("parallel", "parallel"),
        name=name,
    )(x, w)


def _softplus(x):
    return jnp.maximum(x, 0.0) + jnp.log1p(jnp.exp(-jnp.abs(x)))


def _sigmoid(x):
    return 1.0 / (1.0 + jnp.exp(-x))


def _lru_coeffs(xc, wa_ref, wx_ref, ba, bx, lam):
    rs, is_ = [], []
    for n in range(N_LRU_BLOCKS):
        sl = slice(n * LRU_BLOCK, (n + 1) * LRU_BLOCK)
        xb = xc[:, sl].astype(BF16)
        rs.append(jnp.dot(xb, wa_ref[n], preferred_element_type=F32))
        is_.append(jnp.dot(xb, wx_ref[n], preferred_element_type=F32))
    r = _sigmoid(jnp.concatenate(rs, axis=-1) + ba)
    i = _sigmoid(jnp.concatenate(is_, axis=-1) + bx)
    log_a = (-LRU_C * _softplus(-lam)) * r
    a = jnp.exp(log_a)
    u = jnp.sqrt(1.0 - a * a) * (i * xc)
    return a, u


def _lru_prompt_kernel(xr_ref, gr_ref, cw_ref, cb_ref, wa_ref, wx_ref, ba_ref, bx_ref, lam_ref,
                       hs_ref, y_ref, xp_scr, a_scr, u_scr, h_scr):
    tb = ROW_BLOCK

    @pl.when(pl.program_id(0) == 0)
    def _():
        xp_scr[0:8, :] = jnp.zeros((8, D_MODEL), F32)
        h_scr[...] = jnp.zeros((8, D_MODEL), F32)

    x = xr_ref[...]
    xp_scr[8:8 + tb, :] = x
    xc = cb_ref[...] + xp_scr[5:5 + tb, :] * cw_ref[0:1, :]
    xc = xc + xp_scr[6:6 + tb, :] * cw_ref[1:2, :]
    xc = xc + xp_scr[7:7 + tb, :] * cw_ref[2:3, :]
    xc = xc + x * cw_ref[3:4, :]
    xp_scr[0:8, :] = x[tb - 8:tb, :]

    a, u = _lru_coeffs(xc, wa_ref, wx_ref, ba_ref[...], bx_ref[...], lam_ref[...])
    a_scr[...] = a
    u_scr[...] = u

    def group(g, h):
        r0 = pl.multiple_of(g * 8, 8)
        a8 = a_scr[pl.ds(r0, 8), :]
        u8 = u_scr[pl.ds(r0, 8), :]
        rows = []
        for r in range(8):
            h = a8[r:r + 1, :] * h + u8[r:r + 1, :]
            rows.append(h)
        hs_ref[pl.ds(r0, 8), :] = jnp.concatenate(rows, axis=0)
        return h

    h_last = lax.fori_loop(0, tb // 8, group, h_scr[0:1, :])
    h_scr[0:1, :] = h_last

    g = gr_ref[...]
    y_ref[...] = (hs_ref[...] * (g * _sigmoid(g))).astype(y_ref.dtype)


def _lru_prompt(xr, gr, cw, cb, wa, wx, ba, bx, lam, n_rows):
    d = D_MODEL
    row = lambda i: (i, 0)
    const2 = lambda i: (0, 0)
    const3 = lambda i: (0, 0, 0)
    return pl.pallas_call(
        _lru_prompt_kernel,
        grid=(n_rows // ROW_BLOCK,),
        in_specs=[pl.BlockSpec((ROW_BLOCK, d), row), pl.BlockSpec((ROW_BLOCK, d), row),
                  pl.BlockSpec((CONV_W, d), const2), pl.BlockSpec((1, d), const2),
                  pl.BlockSpec((N_LRU_BLOCKS, LRU_BLOCK, LRU_BLOCK), const3),
                  pl.BlockSpec((N_LRU_BLOCKS, LRU_BLOCK, LRU_BLOCK), const3),
                  pl.BlockSpec((1, d), const2), pl.BlockSpec((1, d), const2), pl.BlockSpec((1, d), const2)],
        out_specs=[pl.BlockSpec((ROW_BLOCK, d), row), pl.BlockSpec((ROW_BLOCK, d), row)],
        out_shape=[jax.ShapeDtypeStruct((n_rows, d), F32), jax.ShapeDtypeStruct((n_rows, d), BF16)],
        scratch_shapes=[pltpu.VMEM((8 + ROW_BLOCK, d), F32), pltpu.VMEM((ROW_BLOCK, d), F32),
                        pltpu.VMEM((ROW_BLOCK, d), F32), pltpu.VMEM((8, d), F32)],
        compiler_params=_cparams("arbitrary"),
        name="lru_prompt",
    )(xr, gr, cw, cb, wa, wx, ba, bx, lam)


def _lru_sample_kernel(xs_ref, gr_ref, h0_ref, cw_ref, cb_ref, wa_ref, wx_ref, ba_ref, bx_ref, lam_ref,
                       hs_ref, y_ref):
    n_t = hs_ref.shape[0]
    h = h0_ref[...]
    for t in range(n_t):
        xc = cb_ref[...] + xs_ref[t] * cw_ref[0:1, :]
        for j in range(1, CONV_W):
            xc = xc + xs_ref[t + j] * cw_ref[j:j + 1, :]
        a, u = _lru_coeffs(xc, wa_ref, wx_ref, ba_ref[...], bx_ref[...], lam_ref[...])
        h = a * h + u
        hs_ref[t] = h
        g = gr_ref[t]
        y_ref[t] = (h * (g * _sigmoid(g))).astype(y_ref.dtype)


def _lru_sample(xs, gr, h0, cw, cb, wa, wx, ba, bx, lam):
    n_t, b, d = gr.shape
    return pl.pallas_call(
        _lru_sample_kernel,
        out_shape=[jax.ShapeDtypeStruct((n_t, b, d), F32), jax.ShapeDtypeStruct((n_t, b, d), BF16)],
        compiler_params=pltpu.CompilerParams(vmem_limit_bytes=VMEM_LIMIT),
        name="lru_sample",
    )(xs, gr, h0, cw, cb, wa, wx, ba, bx, lam)


def _order_key(x):
    bits = pltpu.bitcast(x, I32)
    return bits ^ ((bits >> 31) & 0x7FFFFFFF)


def _bisect_threshold(count_ge, zero):
    thr = jnp.where(count_ge(zero) >= TOPK, zero, zero + INT_MIN)

    def body(p, thr):
        cand = thr + jnp.left_shift(jnp.int32(1), 30 - p)
        return jnp.where(count_ge(cand) >= TOPK, cand, thr)

    thr = lax.fori_loop(0, 31, body, thr)
    return jnp.maximum(thr, INT_MIN + 1)


def _prompt_attn_kernel(qit_ref, wt_ref, ki_ref, qt_ref, k_ref, vt_ref, slope_ref, o_ref,
                        key_scr, nd_scr):
    i = pl.program_id(0)
    h = pl.program_id(1)
    n_chunks = i + 1
    t0 = i * Q_BLOCK
    sc, qb = S_CHUNK, Q_BLOCK

    @pl.when(h == 0)
    def _select():
        col = t0 + lax.broadcasted_iota(I32, (sc, qb), 1)

        def score_chunk(c, carry):
            r0 = pl.multiple_of(c * sc, sc)
            kc = ki_ref[pl.ds(r0, sc), :]
            acc = jnp.zeros((sc, qb), F32)
            for hh in range(N_IDX_HEADS):
                s = jnp.dot(kc, qit_ref[hh * IDX_DIM:(hh + 1) * IDX_DIM, :], preferred_element_type=F32)
                acc = acc + jnp.maximum(s, 0.0) * wt_ref[hh:hh + 1, :]
            row = r0 + lax.broadcasted_iota(I32, (sc, qb), 0)
            key_scr[pl.ds(r0, sc), :] = jnp.where(row <= col, _order_key(acc), INT_MIN)
            return carry

        lax.fori_loop(0, n_chunks, score_chunk, 0)

        def count_ge(cand):
            def body(c, cnt):
                r0 = pl.multiple_of(c * sc, sc)
                ind = jnp.where(key_scr[pl.ds(r0, sc), :] >= cand, 1.0, 0.0)
                return cnt + ind.reshape(sc // 8, 8, qb).sum(axis=0)

            cnt8 = lax.fori_loop(0, n_chunks, body, jnp.zeros((8, qb), F32))
            return cnt8.sum(axis=0, keepdims=True)

        thr = _bisect_threshold(count_ge, jnp.zeros((1, qb), I32))

        def nd_chunk(c, carry):
            r0 = pl.multiple_of(c * sc, sc)
            row = r0 + lax.broadcasted_iota(I32, (sc, qb), 0)
            sel = key_scr[pl.ds(r0, sc), :] >= thr
            nd_scr[pl.ds(r0, sc), :] = jnp.where(sel, (row - col).astype(F32), NEG_BIG)
            return carry

        lax.fori_loop(0, n_chunks, nd_chunk, 0)

    qh = qt_ref[...]
    slope = slope_ref[0:1, :]

    def chunk(c, carry):
        m, l, acc = carry
        r0 = pl.multiple_of(c * sc, sc)
        s = jnp.dot(k_ref[pl.ds(r0, sc), :], qh, preferred_element_type=F32)
        s = s + nd_scr[pl.ds(r0, sc), :] * slope
        m_new = jnp.maximum(m, jnp.max(s, axis=0, keepdims=True))
        alpha = jnp.exp(m - m_new)
        p = jnp.exp(s - m_new)
        l = alpha * l + jnp.sum(p, axis=0, keepdims=True)
        acc = alpha * acc + jnp.dot(vt_ref[c], p.astype(BF16), preferred_element_type=F32)
        return m_new, l, acc

    m0 = jnp.full((1, qb), -3e38, F32)
    l0 = jnp.zeros((1, qb), F32)
    acc0 = jnp.zeros((HEAD_DIM, qb), F32)
    _, l, acc = lax.fori_loop(0, n_chunks, chunk, (m0, l0, acc0))
    o_ref[...] = acc / l


def _prompt_attention(qit, wt, ki, qt, k, vt4, slopes, t_pad):
    n_blk = t_pad // Q_BLOCK
    return pl.pallas_call(
        _prompt_attn_kernel,
        grid=(n_blk, N_HEADS),
        in_specs=[pl.BlockSpec((N_IDX_HEADS * IDX_DIM, Q_BLOCK), lambda i, h: (0, i)),
                  pl.BlockSpec((N_IDX_HEADS, Q_BLOCK), lambda i, h: (0, i)),
                  pl.BlockSpec((t_pad, IDX_DIM), lambda i, h: (0, 0)),
                  pl.BlockSpec((HEAD_DIM, Q_BLOCK), lambda i, h: (h, i)),
                  pl.BlockSpec((t_pad, HEAD_DIM), lambda i, h: (0, h)),
                  pl.BlockSpec((None, t_pad // S_CHUNK, HEAD_DIM, S_CHUNK), lambda i, h: (h, 0, 0, 0)),
                  pl.BlockSpec((None, 8, Q_BLOCK), lambda i, h: (h, 0, 0))],
        out_specs=pl.BlockSpec((HEAD_DIM, Q_BLOCK), lambda i, h: (h, i)),
        out_shape=jax.ShapeDtypeStruct((N_HEADS * HEAD_DIM, t_pad), F32),
        scratch_shapes=[pltpu.VMEM((t_pad, Q_BLOCK), I32), pltpu.VMEM((t_pad, Q_BLOCK), F32)],
        compiler_params=_cparams("arbitrary", "arbitrary"),
        name="prompt_attention",
    )(qit, wt, ki, qt, k, vt4, slopes)


def _sample_score_kernel(pt_ref, kidx_ref, kinew_ref, qi_ref, w_ref, o_ref, *, n_pages):
    j = pl.program_id(1)
    page = jnp.where(j < n_pages, kidx_ref[...], kinew_ref[...]).astype(BF16)
    s = lax.dot_general(qi_ref[...], page, (((1,), (1,)), ((), ())), preferred_element_type=F32)
    x = jnp.maximum(s, 0.0) * w_ref[...]
    x8 = x.reshape(N_IDX_HEADS // 2, 8, PAGE).sum(axis=0)
    sc8 = x8 + pltpu.roll(x8, 4, 0)
    slot = lax.broadcasted_iota(I32, (8, PAGE), 1)
    q = lax.broadcasted_iota(I32, (8, PAGE), 0) % 4
    valid = jnp.logical_or(j < n_pages, slot <= q)
    o_ref[...] = jnp.where(valid, sc8, NEG_BIG)


def _sample_scores(page_table, cache_kidx, kinew, qi_s, w_s):
    b, n_pages = page_table.shape
    grid_spec = pltpu.PrefetchScalarGridSpec(
        num_scalar_prefetch=1,
        grid=(b, n_pages + 1),
        in_specs=[pl.BlockSpec((None, None, PAGE, IDX_DIM),
                               lambda s, j, pt: (0, pt[s, jnp.minimum(j, n_pages - 1)], 0, 0)),
                  pl.BlockSpec((None, PAGE, IDX_DIM), lambda s, j, pt: (s, 0, 0)),
                  pl.BlockSpec((None, 4 * N_IDX_HEADS, IDX_DIM), lambda s, j, pt: (s, 0, 0)),
                  pl.BlockSpec((None, 4 * N_IDX_HEADS, PAGE), lambda s, j, pt: (s, 0, 0))],
        out_specs=pl.BlockSpec((None, 8, PAGE), lambda s, j, pt: (s, 0, j)),
    )
    return pl.pallas_call(
        functools.partial(_sample_score_kernel, n_pages=n_pages),
        grid_spec=grid_spec,
        out_shape=jax.ShapeDtypeStruct((b, 8, (n_pages + 1) * PAGE), F32),
        compiler_params=_cparams("arbitrary", "arbitrary"),
        name="sample_scores",
    )(page_table, cache_kidx, kinew, qi_s, w_s)


def _sample_select_kernel(s_ref, o_ref, *, past_len):
    key = _order_key(s_ref[...])
    rows, width = key.shape

    def count_ge(cand):
        return jnp.sum(jnp.where(key >= cand, 1.0, 0.0), axis=1, keepdims=True)

    thr = _bisect_threshold(count_ge, jnp.zeros((rows, 1), I32))
    pos_k = lax.broadcasted_iota(I32, (rows, width), 1)
    pos_q = past_len + lax.broadcasted_iota(I32, (rows, width), 0) % 4
    o_ref[...] = jnp.where(key >= thr, (pos_k - pos_q).astype(F32), NEG_BIG)


def _sample_select(scores, past_len):
    rows, width = scores.shape
    rb = 128
    return pl.pallas_call(
        functools.partial(_sample_select_kernel, past_len=past_len),
        grid=(rows // rb,),
        in_specs=[pl.BlockSpec((rb, width), lambda i: (i, 0))],
        out_specs=pl.BlockSpec((rb, width), lambda i: (i, 0)),
        out_shape=jax.ShapeDtypeStruct((rows, width), F32),
        compiler_params=_cparams("parallel"),
        name="sample_select",
    )(scores)


def _sample_attn_kernel(pt_ref, q_ref, kp_ref, vp_ref, kn_ref, vn_ref, ndp_ref, ndn_ref, slope_ref, o_ref,
                        m_scr, l_scr, acc_scr, *, n_pages):
    j = pl.program_id(1)
    n_rows = 4 * N_HEADS

    @pl.when(j == 0)
    def _():
        m_scr[...] = jnp.full(m_scr.shape, -3e38, F32)
        l_scr[...] = jnp.zeros(l_scr.shape, F32)
        acc_scr[...] = jnp.zeros(acc_scr.shape, F32)

    def step(k2, v2, nd8):
        width = k2.shape[0]
        s = lax.dot_general(q_ref[...], k2.astype(BF16), (((1,), (1,)), ((), ())),
                            preferred_element_type=F32)
        bias = (nd8[None, :, :] * slope_ref[...].reshape(n_rows // 8, 8, 1)).reshape(n_rows, width)
        head_row = lax.broadcasted_iota(I32, (n_rows, width), 0) // 4
        head_col = lax.broadcasted_iota(I32, (n_rows, width), 1) % N_HEADS
        s = jnp.where(head_row == head_col, s + bias, NEG_BIG * 4.0)
        m = m_scr[...]
        m_new = jnp.maximum(m, jnp.max(s, axis=1, keepdims=True))
        alpha = jnp.exp(m - m_new)
        p = jnp.exp(s - m_new)
        l_scr[...] = alpha * l_scr[...] + jnp.sum(p, axis=1, keepdims=True)
        acc_scr[...] = alpha * acc_scr[...] + jnp.dot(p.astype(BF16), v2.astype(BF16),
                                                      preferred_element_type=F32)
        m_scr[...] = m_new

    @pl.when(j < n_pages)
    def _():
        step(kp_ref[...], vp_ref[...], ndp_ref[...])

    @pl.when(j == n_pages)
    def _():
        step(kn_ref[...], vn_ref[...], ndn_ref[...])
        o_ref[...] = acc_scr[...] / l_scr[...]


def _sample_attention(page_table, q_s, cache_k2, cache_v2, knew, vnew, nd_exp, slope_rows):
    b, n_pages = page_table.shape
    pw = PAGE * N_HEADS
    nw = knew.shape[1]
    n_rows = 4 * N_HEADS
    page_idx = lambda s, j, pt: (pt[s, jnp.minimum(j, n_pages - 1)], 0)
    grid_spec = pltpu.PrefetchScalarGridSpec(
        num_scalar_prefetch=1,
        grid=(b, n_pages + 1),
        in_specs=[pl.BlockSpec((None, n_rows, HEAD_DIM), lambda s, j, pt: (s, 0, 0)),
                  pl.BlockSpec((pw, HEAD_DIM), page_idx),
                  pl.BlockSpec((pw, HEAD_DIM), page_idx),
                  pl.BlockSpec((None, nw, HEAD_DIM), lambda s, j, pt: (s, 0, 0)),
                  pl.BlockSpec((None, nw, HEAD_DIM), lambda s, j, pt: (s, 0, 0)),
                  pl.BlockSpec((None, 8, pw), lambda s, j, pt: (s, 0, jnp.minimum(j, n_pages - 1))),
                  pl.BlockSpec((None, 8, nw), lambda s, j, pt: (s, 0, n_pages * pw // nw)),
                  pl.BlockSpec((n_rows, 1), lambda s, j, pt: (0, 0))],
        out_specs=pl.BlockSpec((None, n_rows, HEAD_DIM), lambda s, j, pt: (s, 0, 0)),
        scratch_shapes=[pltpu.VMEM((n_rows, 1), F32), pltpu.VMEM((n_rows, 1), F32),
                        pltpu.VMEM((n_rows, HEAD_DIM), F32)],
    )
    return pl.pallas_call(
        functools.partial(_sample_attn_kernel, n_pages=n_pages),
        grid_spec=grid_spec,
        out_shape=jax.ShapeDtypeStruct((b, n_rows, HEAD_DIM), F32),
        compiler_params=_cparams("arbitrary", "arbitrary"),
        name="sample_attention",
    )(page_table, q_s, cache_k2, cache_v2, knew, vnew, nd_exp, nd_exp, slope_rows)


def _merge_kernel(o_ref, ga_ref, w_ref, pa_ref, g1_ref, g2_ref, out_ref):
    ga = ga_ref[...]
    og = (o_ref[...] * (ga * _sigmoid(ga))).astype(BF16)
    pb = jnp.dot(og, w_ref[...], preferred_element_type=F32)
    merged = _sigmoid(g1_ref[...]) * pa_ref[...] + _sigmoid(g2_ref[...]) * pb
    out_ref[...] = merged.astype(out_ref.dtype)


def _merge(o, ga, w_pb, pa, g1, g2):
    m, d = o.shape
    row = lambda i: (i, 0)
    return pl.pallas_call(
        _merge_kernel,
        grid=(m // ROW_BLOCK,),
        in_specs=[pl.BlockSpec((ROW_BLOCK, d), row), pl.BlockSpec((ROW_BLOCK, d), row),
                  pl.BlockSpec((d, d), lambda i: (0, 0)),
                  pl.BlockSpec((ROW_BLOCK, d), row), pl.BlockSpec((ROW_BLOCK, d), row),
                  pl.BlockSpec((ROW_BLOCK, d), row)],
        out_specs=pl.BlockSpec((ROW_BLOCK, d), row),
        out_shape=jax.ShapeDtypeStruct((m, d), BF16),
        compiler_params=_cparams("parallel"),
        name="merge",
    )(o, ga, w_pb, pa, g1, g2)


def _out_kernel(mg_ref, w_ref, x_ref, g_ref, y_ref):
    res = x_ref[...] + jnp.dot(mg_ref[...], w_ref[...], preferred_element_type=F32)
    y = res * lax.rsqrt(jnp.mean(res * res, axis=-1, keepdims=True) + NORM_EPS)
    y_ref[...] = y * g_ref[...]


def _out_proj(merged, w_out, x, g):
    m, d = x.shape
    row = lambda i: (i, 0)
    return pl.pallas_call(
        _out_kernel,
        grid=(m // ROW_BLOCK,),
        in_specs=[pl.BlockSpec((ROW_BLOCK, d), row), pl.BlockSpec((d, d), lambda i: (0, 0)),
                  pl.BlockSpec((ROW_BLOCK, d), row), pl.BlockSpec((1, d), lambda i: (0, 0))],
        out_specs=pl.BlockSpec((ROW_BLOCK, d), row),
        out_shape=jax.ShapeDtypeStruct((m, d), F32),
        compiler_params=_cparams("parallel"),
        name="out_proj",
    )(merged, w_out, x, g.reshape(1, d))


def kernel(x_prompt, x_sample, cache_k, cache_v, cache_kidx, state_h, state_conv, page_table, meta_tokens,
           norm_g, w_in, conv_w, conv_b, lru_wa, lru_ba, lru_wx, lru_bx, lru_lambda, w_proj_a, w_proj_b,
           w_out, final_g):
    assert x_prompt.shape[0] == 1 and norm_g.shape[0] == 1
    d = D_MODEL
    seq = x_prompt.shape[1]
    t_p = seq + N_META
    t_pad = -(-t_p // Q_BLOCK) * Q_BLOCK
    n_seq, n_new = x_sample.shape[:2]
    n_s = n_seq * n_new
    n_pages = page_table.shape[1]
    past_len = n_pages * PAGE
    assert n_new == 4 and n_s % ROW_BLOCK == 0
    m_all = t_pad + n_s

    x_all = jnp.concatenate([meta_tokens.astype(F32), x_prompt[0], jnp.zeros((t_pad - t_p, d), F32),
                             x_sample.reshape(n_s, d)], axis=0)
    xn = _rmsnorm_bf16(x_all, norm_g[0])

    w = w_in[0]
    cuts = np.cumsum([d, d, d, d, d, d, N_IDX_HEADS * IDX_DIM, IDX_DIM, N_IDX_HEADS, d, d]).tolist()
    wb = lambda lo, hi: w[:, lo:hi].astype(BF16)
    rg = _matmul(xn, wb(0, cuts[1]), F32, name="proj_rnn")
    qv = _matmul(xn, wb(cuts[1], cuts[2]), BF16, scale=HEAD_DIM ** -0.5, name="proj_q")
    kv = _matmul(xn, wb(cuts[2], cuts[4]), F32, name="proj_kv")
    ga = _matmul(xn, wb(cuts[4], cuts[5]), F32, name="proj_ga")
    qi = _matmul(xn, wb(cuts[5], cuts[6]), BF16, scale=IDX_DIM ** -0.5, name="proj_qi")
    w_kw = jnp.pad(w[:, cuts[6]:cuts[8]], ((0, 0), (0, 128 - (cuts[8] - cuts[6])))).astype(BF16)
    kw = _matmul(xn, w_kw, F32, name="proj_kiwi")
    gg = _matmul(xn, wb(cuts[8], cuts[10]), F32, name="proj_gates")

    xr, gr = rg[:, :d], rg[:, d:]
    k_all, v_all = kv[:, :d], kv[:, d:]
    ki_all = kw[:, :IDX_DIM]
    wi_all = kw[:, IDX_DIM:IDX_DIM + N_IDX_HEADS] * (N_IDX_HEADS ** -0.5)

    cw, cb = conv_w[0], conv_b[0].reshape(1, d)
    wa, wx = lru_wa[0].astype(BF16), lru_wx[0].astype(BF16)
    ba, bx, lam = lru_ba[0].reshape(1, d), lru_bx[0].reshape(1, d), lru_lambda[0].reshape(1, d)
    hs_p, y_p = _lru_prompt(xr[:t_pad], gr[:t_pad], cw, cb, wa, wx, ba, bx, lam, t_pad)
    xr_s = jnp.swapaxes(xr[t_pad:].reshape(n_seq, n_new, d), 0, 1)
    gr_s = jnp.swapaxes(gr[t_pad:].reshape(n_seq, n_new, d), 0, 1)
    xs = jnp.concatenate([jnp.swapaxes(state_conv[0], 0, 1), xr_s], axis=0)
    hs_s, y_s = _lru_sample(xs, gr_s, state_h[0], cw, cb, wa, wx, ba, bx, lam)
    y_rnn = jnp.concatenate([y_p, jnp.swapaxes(y_s, 0, 1).reshape(n_s, d)], axis=0)

    slopes = jnp.asarray(2.0 ** (-8.0 * np.arange(1, N_HEADS + 1) / N_HEADS), dtype=F32)
    k_bf, v_bf = k_all.astype(BF16), v_all.astype(BF16)
    qit = qi[:t_pad].T
    wt = wi_all[:t_pad].T
    qt = qv[:t_pad].T
    vt4 = v_bf[:t_pad].reshape(t_pad // S_CHUNK, S_CHUNK, N_HEADS, HEAD_DIM).transpose(2, 0, 3, 1)
    slope_b = jnp.broadcast_to(slopes[:, None, None], (N_HEADS, 8, Q_BLOCK))
    ot = _prompt_attention(qit, wt, ki_all[:t_pad].astype(BF16), qt, k_bf[:t_pad], vt4, slope_b, t_pad)
    o_p = ot.T

    hq = lambda a, width: a.reshape(n_seq, n_new, N_HEADS, width).transpose(0, 2, 1, 3).reshape(
        n_seq, N_HEADS * n_new, width)
    qi_s = hq(qi[t_pad:], IDX_DIM)
    w_s = jnp.broadcast_to(
        wi_all[t_pad:].reshape(n_seq, n_new, N_IDX_HEADS).transpose(0, 2, 1).reshape(n_seq, -1, 1),
        (n_seq, N_IDX_HEADS * n_new, PAGE))
    kinew = jnp.pad(ki_all[t_pad:].reshape(n_seq, n_new, IDX_DIM), ((0, 0), (0, PAGE - n_new), (0, 0)))
    scores = _sample_scores(page_table, cache_kidx, kinew, qi_s, w_s)
    nd_s = _sample_select(scores.reshape(n_seq * 8, past_len + PAGE), past_len)
    nd_exp = jnp.repeat(nd_s[:, :past_len + 8], N_HEADS, axis=1).reshape(n_seq, 8, (past_len + 8) * N_HEADS)
    new_rows = lambda a: jnp.pad(a.reshape(n_seq, n_new * N_HEADS, HEAD_DIM),
                                 ((0, 0), (0, (8 - n_new) * N_HEADS), (0, 0)))
    slope_rows = jnp.repeat(slopes, n_new).reshape(N_HEADS * n_new, 1)
    o_s = _sample_attention(page_table, hq(qv[t_pad:], HEAD_DIM),
                            cache_k.reshape(-1, HEAD_DIM), cache_v.reshape(-1, HEAD_DIM),
                            new_rows(k_all[t_pad:]), new_rows(v_all[t_pad:]), nd_exp, slope_rows)
    o_s = o_s.reshape(n_seq, N_HEADS, n_new, HEAD_DIM).transpose(0, 2, 1, 3).reshape(n_s, d)
    o_all = jnp.concatenate([o_p, o_s], axis=0)

    pa = _matmul(y_rnn, w_proj_a[0].astype(BF16), F32, name="proj_a")
    merged = _merge(o_all, ga, w_proj_b[0].astype(BF16), pa, gg[:, :d], gg[:, d:])
    y_all = _out_proj(merged, w_out[0].astype(BF16), x_all, final_g)

    y_prompt = y_all[N_META:t_p][None]
    y_sample = y_all[t_pad:].reshape(n_seq, n_new, d)
    heads = lambda a, *lead: a.reshape(*lead, N_HEADS, HEAD_DIM)
    k_prompt = heads(k_all[:t_p], 1, 1, t_p)
    v_prompt = heads(v_all[:t_p], 1, 1, t_p)
    kidx_prompt = ki_all[:t_p][None, None]
    h_prompt = hs_p[t_p - 1][None, None]
    conv_prompt = xr[t_p - (CONV_W - 1):t_p][None, None]
    k_sample = heads(k_all[t_pad:], 1, n_seq, n_new)
    v_sample = heads(v_all[t_pad:], 1, n_seq, n_new)
    kidx_sample = ki_all[t_pad:].reshape(1, n_seq, n_new, IDX_DIM)
    h_sample = hs_s[n_new - 1][None]
    conv_sample = jnp.swapaxes(xs[n_new:], 0, 1)[None]
    return (y_prompt, y_sample, k_prompt, v_prompt, kidx_prompt, h_prompt, conv_prompt,
            k_sample, v_sample, kidx_sample, h_sample, conv_sample)
```

```python
import functools

import numpy as np
import jax
import jax.numpy as jnp
from jax import lax
from jax.experimental import pallas as pl
from jax.experimental.pallas import tpu as pltpu

F32 = jnp.float32
BF16 = jnp.bfloat16
I32 = jnp.int32

D_MODEL = 2048
N_HEADS = 16
HEAD_DIM = 128
N_IDX_HEADS = 16
IDX_DIM = 64
N_LRU_BLOCKS = 16
LRU_BLOCK = 128
CONV_W = 4
LRU_C = 8.0
N_META = 16
TOPK = 256
PAGE = 128
NORM_EPS = 1e-6
NEG_BIG = -1e30
INT_MIN = -(2 ** 31)

ROW_BLOCK = 256
Q_BLOCK = 256
S_CHUNK = 256
BIG_CHUNK = 1024
LOG2E = 1.4426950408889634
VMEM_LIMIT = 56 * 1024 * 1024


def _cparams(*sem):
    return pltpu.CompilerParams(dimension_semantics=sem, vmem_limit_bytes=VMEM_LIMIT)


def _rmsnorm_kernel(x_ref, g_ref, o_ref):
    x = x_ref[...]
    y = x * lax.rsqrt(jnp.mean(x * x, axis=-1, keepdims=True) + NORM_EPS)
    o_ref[...] = (y * g_ref[...]).astype(o_ref.dtype)


def _rmsnorm_bf16(x, g):
    m, d = x.shape
    return pl.pallas_call(
        _rmsnorm_kernel,
        grid=(m // ROW_BLOCK,),
        in_specs=[pl.BlockSpec((ROW_BLOCK, d), lambda i: (i, 0)),
                  pl.BlockSpec((1, d), lambda i: (0, 0))],
        out_specs=pl.BlockSpec((ROW_BLOCK, d), lambda i: (i, 0)),
        out_shape=jax.ShapeDtypeStruct((m, d), BF16),
        compiler_params=_cparams("parallel"),
        name="rmsnorm_in",
    )(x, g.reshape(1, d))


def _mm_kernel(x_ref, w_ref, o_ref, *, scale):
    acc = jnp.dot(x_ref[...], w_ref[...], preferred_element_type=F32)
    if scale is not None:
        acc = acc * scale
    o_ref[...] = acc.astype(o_ref.dtype)


def _matmul(x, w, out_dtype, *, scale=None, tm=1280, tn=1024, name="matmul"):
    m, k = x.shape
    n = w.shape[1]
    tn = min(tn, n)
    assert m % tm == 0 and n % tn == 0
    return pl.pallas_call(
        functools.partial(_mm_kernel, scale=scale),
        grid=(n // tn, m // tm),
        in_specs=[pl.BlockSpec((tm, k), lambda j, i: (i, 0)),
                  pl.BlockSpec((k, tn), lambda j, i: (0, j))],
        out_specs=pl.BlockSpec((tm, tn), lambda j, i: (i, j)),
        out_shape=jax.ShapeDtypeStruct((m, n), out_dtype),
        compiler_params=_cparams("parallel", "parallel"),
        name=name,
    )(x, w)


def _softplus(x):
    return jnp.maximum(x, 0.0) + jnp.log1p(jnp.exp(-jnp.abs(x)))


def _sigmoid(x):
    return 1.0 / (1.0 + jnp.exp(-x))


def _lru_coeffs(xc, wa_ref, wx_ref, ba, bx, lam):
    rs, is_ = [], []
    for n in range(N_LRU_BLOCKS):
        sl = slice(n * LRU_BLOCK, (n + 1) * LRU_BLOCK)
        xb = xc[:, sl].astype(BF16)
        rs.append(jnp.dot(xb, wa_ref[n], preferred_element_type=F32))
        is_.append(jnp.dot(xb, wx_ref[n], preferred_element_type=F32))
    r = _sigmoid(jnp.concatenate(rs, axis=-1) + ba)
    i = _sigmoid(jnp.concatenate(is_, axis=-1) + bx)
    log_a = (-LRU_C * _softplus(-lam)) * r
    a = jnp.exp(log_a)
    u = jnp.sqrt(1.0 - a * a) * (i * xc)
    return a, u


def _lru_prompt_kernel(xr_ref, gr_ref, cw_ref, cb_ref, wa_ref, wx_ref, ba_ref, bx_ref, lam_ref,
                       hs_ref, y_ref, xp_scr, a_scr, u_scr, h_scr):
    tb = ROW_BLOCK

    @pl.when(pl.program_id(0) == 0)
    def _():
        xp_scr[0:8, :] = jnp.zeros((8, D_MODEL), F32)
        h_scr[...] = jnp.zeros((8, D_MODEL), F32)

    x = xr_ref[...]
    xp_scr[8:8 + tb, :] = x
    xc = cb_ref[...] + xp_scr[5:5 + tb, :] * cw_ref[0:1, :]
    xc = xc + xp_scr[6:6 + tb, :] * cw_ref[1:2, :]
    xc = xc + xp_scr[7:7 + tb, :] * cw_ref[2:3, :]
    xc = xc + x * cw_ref[3:4, :]
    xp_scr[0:8, :] = x[tb - 8:tb, :]

    a, u = _lru_coeffs(xc, wa_ref, wx_ref, ba_ref[...], bx_ref[...], lam_ref[...])
    a_scr[...] = a
    u_scr[...] = u

    def group(g, h):
        r0 = pl.multiple_of(g * 8, 8)
        a8 = a_scr[pl.ds(r0, 8), :]
        u8 = u_scr[pl.ds(r0, 8), :]
        rows = []
        for r in range(8):
            h = a8[r:r + 1, :] * h + u8[r:r + 1, :]
            rows.append(h)
        hs_ref[pl.ds(r0, 8), :] = jnp.concatenate(rows, axis=0)
        return h

    h_last = lax.fori_loop(0, tb // 8, group, h_scr[0:1, :])
    h_scr[0:1, :] = h_last

    g = gr_ref[...]
    y_ref[...] = (hs_ref[...] * (g * _sigmoid(g))).astype(y_ref.dtype)


def _lru_prompt(xr, gr, cw, cb, wa, wx, ba, bx, lam, n_rows):
    d = D_MODEL
    row = lambda i: (i, 0)
    const2 = lambda i: (0, 0)
    const3 = lambda i: (0, 0, 0)
    return pl.pallas_call(
        _lru_prompt_kernel,
        grid=(n_rows // ROW_BLOCK,),
        in_specs=[pl.BlockSpec((ROW_BLOCK, d), row), pl.BlockSpec((ROW_BLOCK, d), row),
                  pl.BlockSpec((CONV_W, d), const2), pl.BlockSpec((1, d), const2),
                  pl.BlockSpec((N_LRU_BLOCKS, LRU_BLOCK, LRU_BLOCK), const3),
                  pl.BlockSpec((N_LRU_BLOCKS, LRU_BLOCK, LRU_BLOCK), const3),
                  pl.BlockSpec((1, d), const2), pl.BlockSpec((1, d), const2), pl.BlockSpec((1, d), const2)],
        out_specs=[pl.BlockSpec((ROW_BLOCK, d), row), pl.BlockSpec((ROW_BLOCK, d), row)],
        out_shape=[jax.ShapeDtypeStruct((n_rows, d), F32), jax.ShapeDtypeStruct((n_rows, d), BF16)],
        scratch_shapes=[pltpu.VMEM((8 + ROW_BLOCK, d), F32), pltpu.VMEM((ROW_BLOCK, d), F32),
                        pltpu.VMEM((ROW_BLOCK, d), F32), pltpu.VMEM((8, d), F32)],
        compiler_params=_cparams("arbitrary"),
        name="lru_prompt",
    )(xr, gr, cw, cb, wa, wx, ba, bx, lam)


def _lru_sample_kernel(xs_ref, gr_ref, h0_ref, cw_ref, cb_ref, wa_ref, wx_ref, ba_ref, bx_ref, lam_ref,
                       hs_ref, y_ref):
    n_t = hs_ref.shape[0]
    h = h0_ref[...]
    for t in range(n_t):
        xc = cb_ref[...] + xs_ref[t] * cw_ref[0:1, :]
        for j in range(1, CONV_W):
            xc = xc + xs_ref[t + j] * cw_ref[j:j + 1, :]
        a, u = _lru_coeffs(xc, wa_ref, wx_ref, ba_ref[...], bx_ref[...], lam_ref[...])
        h = a * h + u
        hs_ref[t] = h
        g = gr_ref[t]
        y_ref[t] = (h * (g * _sigmoid(g))).astype(y_ref.dtype)


def _lru_sample(xs, gr, h0, cw, cb, wa, wx, ba, bx, lam):
    n_t, b, d = gr.shape
    return pl.pallas_call(
        _lru_sample_kernel,
        out_shape=[jax.ShapeDtypeStruct((n_t, b, d), F32), jax.ShapeDtypeStruct((n_t, b, d), BF16)],
        compiler_params=pltpu.CompilerParams(vmem_limit_bytes=VMEM_LIMIT),
        name="lru_sample",
    )(xs, gr, h0, cw, cb, wa, wx, ba, bx, lam)


LOWEST_KEY = INT_MIN + 0x00800000


def _key_to_f32(key):
    return pltpu.bitcast(key ^ ((key >> 31) & 0x7FFFFFFF), F32)


def _bisect_threshold(count_ge, zero):
    thr = jnp.where(count_ge(jnp.zeros(zero.shape, F32)) >= TOPK, zero, zero + INT_MIN)

    def body(p, thr):
        cand = thr + jnp.left_shift(jnp.int32(1), 30 - p)
        return jnp.where(count_ge(_key_to_f32(cand)) >= TOPK, cand, thr)

    thr = lax.fori_loop(0, 31, body, thr)
    return _key_to_f32(jnp.maximum(thr, LOWEST_KEY))


def _for_key_rows(n_small, body, carry):
    per_big = BIG_CHUNK // S_CHUNK
    n_big = n_small // per_big
    carry = lax.fori_loop(
        0, n_big, lambda c, x: body(pl.multiple_of(c * BIG_CHUNK, BIG_CHUNK), BIG_CHUNK, x), carry)
    return lax.fori_loop(
        n_big * per_big, n_small, lambda c, x: body(pl.multiple_of(c * S_CHUNK, S_CHUNK), S_CHUNK, x), carry)


def _prompt_attn_kernel(qit_ref, wt_ref, ki_ref, qt_ref, k_ref, vt_ref, slope_ref, o_ref, sel_scr, s_scr):
    i = pl.program_id(0)
    h = pl.program_id(1)
    n_chunks = i + 1
    t0 = i * Q_BLOCK
    sc, qb = S_CHUNK, Q_BLOCK

    @pl.when(h == 0)
    def _select():
        def score_chunk(c, carry):
            r0 = pl.multiple_of(c * sc, sc)
            kc = ki_ref[pl.ds(r0, sc), :]
            acc = jnp.zeros((sc, qb), F32)
            for hh in range(N_IDX_HEADS):
                s = jnp.dot(kc, qit_ref[hh * IDX_DIM:(hh + 1) * IDX_DIM, :], preferred_element_type=F32)
                acc = acc + jnp.maximum(s, 0.0) * wt_ref[hh:hh + 1, :]
            row = r0 + lax.broadcasted_iota(I32, (sc, qb), 0)
            col = t0 + lax.broadcasted_iota(I32, (sc, qb), 1)
            sel_scr[pl.ds(r0, sc), :] = jnp.where(row <= col, acc, -jnp.inf)
            return carry

        lax.fori_loop(0, n_chunks, score_chunk, 0)

        def count_ge(cand):
            def body(r0, rows, cnt):
                ind = jnp.where(sel_scr[pl.ds(r0, rows), :] >= cand, 1.0, 0.0)
                return cnt + ind.reshape(rows // 8, 8, qb).sum(axis=0)

            cnt8 = _for_key_rows(n_chunks, body, jnp.zeros((8, qb), F32))
            return cnt8.sum(axis=0, keepdims=True)

        thr = _bisect_threshold(count_ge, jnp.zeros((1, qb), I32))

        def mark(r0, rows, carry):
            row = r0 + lax.broadcasted_iota(I32, (rows, qb), 0)
            col = t0 + lax.broadcasted_iota(I32, (rows, qb), 1)
            sel = sel_scr[pl.ds(r0, rows), :] >= thr
            sel_scr[pl.ds(r0, rows), :] = jnp.where(sel, (row - col).astype(F32), NEG_BIG)
            return carry

        _for_key_rows(n_chunks, mark, 0)

    qh = qt_ref[...]
    slope = slope_ref[0:1, :]

    def logits(r0, rows, mx):
        s = jnp.dot(k_ref[pl.ds(r0, rows), :], qh, preferred_element_type=F32)
        s = s + sel_scr[pl.ds(r0, rows), :] * slope
        s_scr[pl.ds(r0, rows), :] = s
        return jnp.maximum(mx, s.reshape(rows // 8, 8, qb).max(axis=0))

    mx8 = _for_key_rows(n_chunks, logits, jnp.full((8, qb), -3e38, F32))
    m = mx8.max(axis=0, keepdims=True)

    def weigh(r0, rows, carry):
        l8, acc = carry
        p = jnp.exp2(s_scr[pl.ds(r0, rows), :] - m)
        l8 = l8 + p.reshape(rows // 8, 8, qb).sum(axis=0)
        c0 = r0 // sc
        v = jnp.concatenate([vt_ref[c0 + u] for u in range(rows // sc)], axis=1)
        acc = acc + jnp.dot(v, p.astype(BF16), preferred_element_type=F32)
        return l8, acc

    l8, acc = _for_key_rows(n_chunks, weigh, (jnp.zeros((8, qb), F32), jnp.zeros((HEAD_DIM, qb), F32)))
    o_ref[...] = acc / l8.sum(axis=0, keepdims=True)


def _prompt_attention(qit, wt, ki, qt, k, vt4, slopes, t_pad):
    n_blk = t_pad // Q_BLOCK
    return pl.pallas_call(
        _prompt_attn_kernel,
        grid=(n_blk, N_HEADS),
        in_specs=[pl.BlockSpec((N_IDX_HEADS * IDX_DIM, Q_BLOCK), lambda i, h: (0, i)),
                  pl.BlockSpec((N_IDX_HEADS, Q_BLOCK), lambda i, h: (0, i)),
                  pl.BlockSpec((t_pad, IDX_DIM), lambda i, h: (0, 0)),
                  pl.BlockSpec((HEAD_DIM, Q_BLOCK), lambda i, h: (h, i)),
                  pl.BlockSpec((t_pad, HEAD_DIM), lambda i, h: (0, h)),
                  pl.BlockSpec((None, t_pad // S_CHUNK, HEAD_DIM, S_CHUNK), lambda i, h: (h, 0, 0, 0)),
                  pl.BlockSpec((None, 8, Q_BLOCK), lambda i, h: (h, 0, 0))],
        out_specs=pl.BlockSpec((HEAD_DIM, Q_BLOCK), lambda i, h: (h, i)),
        out_shape=jax.ShapeDtypeStruct((N_HEADS * HEAD_DIM, t_pad), F32),
        scratch_shapes=[pltpu.VMEM((t_pad, Q_BLOCK), F32), pltpu.VMEM((t_pad, Q_BLOCK), F32)],
        compiler_params=_cparams("arbitrary", "arbitrary"),
        name="prompt_attention",
    )(qit, wt, ki, qt, k, vt4, slopes)


N_SROWS = 4 * N_HEADS
PAGE_ROWS = PAGE * N_HEADS
PAGES_PER_STEP = 4


def _sample_score_kernel(pt_ref, *refs, n_pages):
    kidx_refs = refs[:n_pages]
    kinew_ref, qi_ref, w_ref, o_ref = refs[n_pages:]
    qi = qi_ref[...]
    w = w_ref[...]

    def page_scores(page):
        s = lax.dot_general(qi, page.astype(BF16), (((1,), (1,)), ((), ())), preferred_element_type=F32)
        x = jnp.maximum(s, 0.0) * w
        x8 = x.reshape(N_IDX_HEADS // 2, 8, PAGE).sum(axis=0)
        return x8 + pltpu.roll(x8, 4, 0)

    for p in range(n_pages):
        o_ref[:, p * PAGE:(p + 1) * PAGE] = page_scores(kidx_refs[p][...])
    slot = lax.broadcasted_iota(I32, (8, PAGE), 1)
    q = lax.broadcasted_iota(I32, (8, PAGE), 0) % 4
    o_ref[:, n_pages * PAGE:] = jnp.where(slot <= q, page_scores(kinew_ref[...]), NEG_BIG)


def _sample_scores(page_table, cache_kidx, kinew, qi_s, w_s):
    b, n_pages = page_table.shape

    def page_index(s, pt, *, p):
        return (0, pt[s, p], 0, 0)

    grid_spec = pltpu.PrefetchScalarGridSpec(
        num_scalar_prefetch=1,
        grid=(b,),
        in_specs=[pl.BlockSpec((None, None, PAGE, IDX_DIM), functools.partial(page_index, p=p))
                  for p in range(n_pages)]
        + [pl.BlockSpec((None, PAGE, IDX_DIM), lambda s, pt: (s, 0, 0)),
           pl.BlockSpec((None, N_SROWS, IDX_DIM), lambda s, pt: (s, 0, 0)),
           pl.BlockSpec((None, N_SROWS, PAGE), lambda s, pt: (s, 0, 0))],
        out_specs=pl.BlockSpec((None, 8, (n_pages + 1) * PAGE), lambda s, pt: (s, 0, 0)),
    )
    return pl.pallas_call(
        functools.partial(_sample_score_kernel, n_pages=n_pages),
        grid_spec=grid_spec,
        out_shape=jax.ShapeDtypeStruct((b, 8, (n_pages + 1) * PAGE), F32),
        compiler_params=_cparams("arbitrary"),
        name="sample_scores",
    )(page_table, *([cache_kidx] * n_pages), kinew, qi_s, w_s)


def _sample_select_kernel(s_ref, e_ref, o_ref):
    sc = s_ref[...]
    rows, width = sc.shape

    def count_ge(cand):
        return jnp.sum(jnp.where(sc >= cand, 1.0, 0.0), axis=1, keepdims=True)

    thr = _bisect_threshold(count_ge, jnp.zeros((rows, 1), I32))
    sel = jnp.where(sc >= thr, 1.0, 0.0).astype(BF16)
    n_tiles = width // PAGE
    for t in range(n_tiles):
        x = jnp.dot(sel[:, t * PAGE:(t + 1) * PAGE], e_ref[...], preferred_element_type=F32)
        lo = t * PAGE_ROWS
        hi = min(lo + PAGE_ROWS, o_ref.shape[1])
        o_ref[:, lo:hi] = x[:, :hi - lo]


def _sample_select(scores, expand, out_width):
    rows, width = scores.shape
    rb = 64
    return pl.pallas_call(
        _sample_select_kernel,
        grid=(rows // rb,),
        in_specs=[pl.BlockSpec((rb, width), lambda i: (i, 0)),
                  pl.BlockSpec((PAGE, PAGE_ROWS), lambda i: (0, 0))],
        out_specs=pl.BlockSpec((rb, out_width), lambda i: (i, 0)),
        out_shape=jax.ShapeDtypeStruct((rows, out_width), F32),
        compiler_params=_cparams("parallel"),
        name="sample_select",
    )(scores, expand)


def _sample_attn_kernel(pt_ref, q_ref, *refs, n_steps, past_len):
    pps = PAGES_PER_STEP
    k_refs, v_refs = refs[:pps], refs[pps:2 * pps]
    kn_ref, vn_ref, selp_ref, seln_ref, slope_ref, hm_ref, o_ref, m_scr, l_scr, acc_scr = refs[2 * pps:]
    jj = pl.program_id(1)

    @pl.when(jj == 0)
    def _():
        m_scr[...] = jnp.full(m_scr.shape, -3e38, F32)
        l_scr[...] = jnp.zeros(l_scr.shape, F32)
        acc_scr[...] = jnp.zeros(acc_scr.shape, F32)

    def step(k_list, v_list, sel8, pos0):
        q = q_ref[...]
        s = jnp.concatenate(
            [lax.dot_general(q, k.astype(BF16), (((1,), (1,)), ((), ())), preferred_element_type=F32)
             for k in k_list], axis=1)
        width = s.shape[1]
        lane = lax.broadcasted_iota(I32, (8, width), 1)
        rq = lax.broadcasted_iota(I32, (8, width), 0) % 4
        dist = (pos0 - past_len + lane // N_HEADS - rq).astype(F32)
        hm = jnp.concatenate([hm_ref[...]] * (width // PAGE_ROWS), axis=1) if width > PAGE_ROWS \
            else hm_ref[:, :width]
        g = N_SROWS // 8
        x = s.reshape(g, 8, width) + slope_ref[...].reshape(g, 8, 1) * dist[None]
        x = jnp.where(sel8[None] > 0.5, x, NEG_BIG) + hm.reshape(g, 8, width)
        x = x.reshape(N_SROWS, width)
        m = m_scr[...]
        m_new = jnp.maximum(m, jnp.max(x, axis=1, keepdims=True))
        alpha = jnp.exp2(m - m_new)
        p = jnp.exp2(x - m_new)
        l_scr[...] = alpha * l_scr[...] + jnp.sum(p, axis=1, keepdims=True)
        pb = p.astype(BF16)
        pv = None
        off = 0
        for v in v_list:
            part = jnp.dot(pb[:, off:off + v.shape[0]], v.astype(BF16), preferred_element_type=F32)
            pv = part if pv is None else pv + part
            off += v.shape[0]
        acc_scr[...] = alpha * acc_scr[...] + pv
        m_scr[...] = m_new

    @pl.when(jj < n_steps)
    def _():
        step([r[...] for r in k_refs], [r[...] for r in v_refs], selp_ref[...], jj * (pps * PAGE))

    @pl.when(jj == n_steps)
    def _():
        step([kn_ref[...]], [vn_ref[...]], seln_ref[...], past_len)
        o_ref[...] = acc_scr[...] / l_scr[...]


def _sample_attention(page_table, q_s, cache_k2, cache_v2, knew, vnew, sel_exp, slope_rows, head_mask):
    b, n_pages = page_table.shape
    pps = PAGES_PER_STEP
    assert n_pages % pps == 0
    n_steps = n_pages // pps
    nw = knew.shape[1]

    def page_index(s, jj, pt, *, u):
        return (pt[s, jnp.minimum(jj, n_steps - 1) * pps + u], 0)

    page_specs = [pl.BlockSpec((PAGE_ROWS, HEAD_DIM), functools.partial(page_index, u=u)) for u in range(pps)]
    grid_spec = pltpu.PrefetchScalarGridSpec(
        num_scalar_prefetch=1,
        grid=(b, n_steps + 1),
        in_specs=[pl.BlockSpec((None, N_SROWS, HEAD_DIM), lambda s, jj, pt: (s, 0, 0))]
        + page_specs + page_specs
        + [pl.BlockSpec((None, nw, HEAD_DIM), lambda s, jj, pt: (s, 0, 0)),
           pl.BlockSpec((None, nw, HEAD_DIM), lambda s, jj, pt: (s, 0, 0)),
           pl.BlockSpec((None, 8, pps * PAGE_ROWS), lambda s, jj, pt: (s, 0, jnp.minimum(jj, n_steps - 1))),
           pl.BlockSpec((None, 8, nw), lambda s, jj, pt: (s, 0, n_pages * PAGE_ROWS // nw)),
           pl.BlockSpec((N_SROWS, 1), lambda s, jj, pt: (0, 0)),
           pl.BlockSpec((N_SROWS, PAGE_ROWS), lambda s, jj, pt: (0, 0))],
        out_specs=pl.BlockSpec((None, N_SROWS, HEAD_DIM), lambda s, jj, pt: (s, 0, 0)),
        scratch_shapes=[pltpu.VMEM((N_SROWS, 1), F32), pltpu.VMEM((N_SROWS, 1), F32),
                        pltpu.VMEM((N_SROWS, HEAD_DIM), F32)],
    )
    return pl.pallas_call(
        functools.partial(_sample_attn_kernel, n_steps=n_steps, past_len=n_pages * PAGE),
        grid_spec=grid_spec,
        out_shape=jax.ShapeDtypeStruct((b, N_SROWS, HEAD_DIM), F32),
        compiler_params=_cparams("arbitrary", "arbitrary"),
        name="sample_attention",
    )(page_table, q_s, *([cache_k2] * pps), *([cache_v2] * pps), knew, vnew, sel_exp, sel_exp, slope_rows,
      head_mask)


def _merge_kernel(o_ref, ga_ref, w_ref, pa_ref, g1_ref, g2_ref, out_ref):
    ga = ga_ref[...]
    og = (o_ref[...] * (ga * _sigmoid(ga))).astype(BF16)
    pb = jnp.dot(og, w_ref[...], preferred_element_type=F32)
    merged = _sigmoid(g1_ref[...]) * pa_ref[...] + _sigmoid(g2_ref[...]) * pb
    out_ref[...] = merged.astype(out_ref.dtype)


def _merge(o, ga, w_pb, pa, g1, g2):
    m, d = o.shape
    row = lambda i: (i, 0)
    return pl.pallas_call(
        _merge_kernel,
        grid=(m // ROW_BLOCK,),
        in_specs=[pl.BlockSpec((ROW_BLOCK, d), row), pl.BlockSpec((ROW_BLOCK, d), row),
                  pl.BlockSpec((d, d), lambda i: (0, 0)),
                  pl.BlockSpec((ROW_BLOCK, d), row), pl.BlockSpec((ROW_BLOCK, d), row),
                  pl.BlockSpec((ROW_BLOCK, d), row)],
        out_specs=pl.BlockSpec((ROW_BLOCK, d), row),
        out_shape=jax.ShapeDtypeStruct((m, d), BF16),
        compiler_params=_cparams("parallel"),
        name="merge",
    )(o, ga, w_pb, pa, g1, g2)


def _out_kernel(mg_ref, w_ref, x_ref, g_ref, y_ref):
    res = x_ref[...] + jnp.dot(mg_ref[...], w_ref[...], preferred_element_type=F32)
    y = res * lax.rsqrt(jnp.mean(res * res, axis=-1, keepdims=True) + NORM_EPS)
    y_ref[...] = y * g_ref[...]


def _out_proj(merged, w_out, x, g):
    m, d = x.shape
    row = lambda i: (i, 0)
    return pl.pallas_call(
        _out_kernel,
        grid=(m // ROW_BLOCK,),
        in_specs=[pl.BlockSpec((ROW_BLOCK, d), row), pl.BlockSpec((d, d), lambda i: (0, 0)),
                  pl.BlockSpec((ROW_BLOCK, d), row), pl.BlockSpec((1, d), lambda i: (0, 0))],
        out_specs=pl.BlockSpec((ROW_BLOCK, d), row),
        out_shape=jax.ShapeDtypeStruct((m, d), F32),
        compiler_params=_cparams("parallel"),
        name="out_proj",
    )(merged, w_out, x, g.reshape(1, d))


def kernel(x_prompt, x_sample, cache_k, cache_v, cache_kidx, state_h, state_conv, page_table, meta_tokens,
           norm_g, w_in, conv_w, conv_b, lru_wa, lru_ba, lru_wx, lru_bx, lru_lambda, w_proj_a, w_proj_b,
           w_out, final_g):
    assert x_prompt.shape[0] == 1 and norm_g.shape[0] == 1
    d = D_MODEL
    seq = x_prompt.shape[1]
    t_p = seq + N_META
    t_pad = -(-t_p // Q_BLOCK) * Q_BLOCK
    n_seq, n_new = x_sample.shape[:2]
    n_s = n_seq * n_new
    n_pages = page_table.shape[1]
    past_len = n_pages * PAGE
    assert n_new == 4 and n_s % ROW_BLOCK == 0
    m_all = t_pad + n_s

    x_all = jnp.concatenate([meta_tokens.astype(F32), x_prompt[0], jnp.zeros((t_pad - t_p, d), F32),
                             x_sample.reshape(n_s, d)], axis=0)
    xn = _rmsnorm_bf16(x_all, norm_g[0])

    w = w_in[0]
    cuts = np.cumsum([d, d, d, d, d, d, N_IDX_HEADS * IDX_DIM, IDX_DIM, N_IDX_HEADS, d, d]).tolist()
    wb = lambda lo, hi: w[:, lo:hi].astype(BF16)
    rg = _matmul(xn, wb(0, cuts[1]), F32, name="proj_rnn")
    qv = _matmul(xn, wb(cuts[1], cuts[2]), BF16, scale=LOG2E * HEAD_DIM ** -0.5, name="proj_q")
    kv = _matmul(xn, wb(cuts[2], cuts[4]), F32, name="proj_kv")
    ga = _matmul(xn, wb(cuts[4], cuts[5]), F32, name="proj_ga")
    qi = _matmul(xn, wb(cuts[5], cuts[6]), BF16, scale=IDX_DIM ** -0.5, name="proj_qi")
    w_kw = jnp.pad(w[:, cuts[6]:cuts[8]], ((0, 0), (0, 128 - (cuts[8] - cuts[6])))).astype(BF16)
    kw = _matmul(xn, w_kw, F32, name="proj_kiwi")
    gg = _matmul(xn, wb(cuts[8], cuts[10]), F32, name="proj_gates")

    xr, gr = rg[:, :d], rg[:, d:]
    k_all, v_all = kv[:, :d], kv[:, d:]
    ki_all = kw[:, :IDX_DIM]
    wi_all = kw[:, IDX_DIM:IDX_DIM + N_IDX_HEADS] * (N_IDX_HEADS ** -0.5)

    cw, cb = conv_w[0], conv_b[0].reshape(1, d)
    wa, wx = lru_wa[0].astype(BF16), lru_wx[0].astype(BF16)
    ba, bx, lam = lru_ba[0].reshape(1, d), lru_bx[0].reshape(1, d), lru_lambda[0].reshape(1, d)
    hs_p, y_p = _lru_prompt(xr[:t_pad], gr[:t_pad], cw, cb, wa, wx, ba, bx, lam, t_pad)
    xr_s = jnp.swapaxes(xr[t_pad:].reshape(n_seq, n_new, d), 0, 1)
    gr_s = jnp.swapaxes(gr[t_pad:].reshape(n_seq, n_new, d), 0, 1)
    xs = jnp.concatenate([jnp.swapaxes(state_conv[0], 0, 1), xr_s], axis=0)
    hs_s, y_s = _lru_sample(xs, gr_s, state_h[0], cw, cb, wa, wx, ba, bx, lam)
    y_rnn = jnp.concatenate([y_p, jnp.swapaxes(y_s, 0, 1).reshape(n_s, d)], axis=0)

    slopes = jnp.asarray(LOG2E * 2.0 ** (-8.0 * np.arange(1, N_HEADS + 1) / N_HEADS), dtype=F32)
    k_bf, v_bf = k_all.astype(BF16), v_all.astype(BF16)
    qit = qi[:t_pad].T
    wt = wi_all[:t_pad].T
    qt = qv[:t_pad].T
    vt4 = v_bf[:t_pad].reshape(t_pad // S_CHUNK, S_CHUNK, N_HEADS, HEAD_DIM).transpose(2, 0, 3, 1)
    slope_b = jnp.broadcast_to(slopes[:, None, None], (N_HEADS, 8, Q_BLOCK))
    ot = _prompt_attention(qit, wt, ki_all[:t_pad].astype(BF16), qt, k_bf[:t_pad], vt4, slope_b, t_pad)
    o_p = ot.T

    hq = lambda a, width: a.reshape(n_seq, n_new, N_HEADS, width).transpose(0, 2, 1, 3).reshape(
        n_seq, N_HEADS * n_new, width)
    qi_s = hq(qi[t_pad:], IDX_DIM)
    w_s = jnp.broadcast_to(
        wi_all[t_pad:].reshape(n_seq, n_new, N_IDX_HEADS).transpose(0, 2, 1).reshape(n_seq, -1, 1),
        (n_seq, N_IDX_HEADS * n_new, PAGE))
    kinew = jnp.pad(ki_all[t_pad:].reshape(n_seq, n_new, IDX_DIM), ((0, 0), (0, PAGE - n_new), (0, 0)))
    scores = _sample_scores(page_table, cache_kidx, kinew, qi_s, w_s)
    col = np.arange(PAGE_ROWS)
    expand = jnp.asarray(col[None, :] // N_HEADS == np.arange(PAGE)[:, None], dtype=BF16)
    head_mask = jnp.asarray(np.where(np.arange(N_SROWS)[:, None] // n_new == col[None, :] % N_HEADS,
                                     0.0, 4.0 * NEG_BIG), dtype=F32)
    sel_exp = _sample_select(scores.reshape(n_seq * 8, past_len + PAGE), expand, (past_len + 8) * N_HEADS)
    sel_exp = sel_exp.reshape(n_seq, 8, (past_len + 8) * N_HEADS)
    new_rows = lambda a: jnp.pad(a.reshape(n_seq, n_new * N_HEADS, HEAD_DIM),
                                 ((0, 0), (0, (8 - n_new) * N_HEADS), (0, 0)))
    slope_rows = jnp.repeat(slopes, n_new).reshape(N_SROWS, 1)
    o_s = _sample_attention(page_table, hq(qv[t_pad:], HEAD_DIM),
                            cache_k.reshape(-1, HEAD_DIM), cache_v.reshape(-1, HEAD_DIM),
                            new_rows(k_all[t_pad:]), new_rows(v_all[t_pad:]), sel_exp, slope_rows, head_mask)
    o_s = o_s.reshape(n_seq, N_HEADS, n_new, HEAD_DIM).transpose(0, 2, 1, 3).reshape(n_s, d)
    o_all = jnp.concatenate([o_p, o_s], axis=0)

    pa = _matmul(y_rnn, w_proj_a[0].astype(BF16), F32, name="proj_a")
    merged = _merge(o_all, ga, w_proj_b[0].astype(BF16), pa, gg[:, :d], gg[:, d:])
    y_all = _out_proj(merged, w_out[0].astype(BF16), x_all, final_g)

    y_prompt = y_all[N_META:t_p][None]
    y_sample = y_all[t_pad:].reshape(n_seq, n_new, d)
    heads = lambda a, *lead: a.reshape(*lead, N_HEADS, HEAD_DIM)
    k_prompt = heads(k_all[:t_p], 1, 1, t_p)
    v_prompt = heads(v_all[:t_p], 1, 1, t_p)
    kidx_prompt = ki_all[:t_p][None, None]
    h_prompt = hs_p[t_p - 1][None, None]
    conv_prompt = xr[t_p - (CONV_W - 1):t_p][None, None]
    k_sample = heads(k_all[t_pad:], 1, n_seq, n_new)
    v_sample = heads(v_all[t_pad:], 1, n_seq, n_new)
    kidx_sample = ki_all[t_pad:].reshape(1, n_seq, n_new, IDX_DIM)
    h_sample = hs_s[n_new - 1][None]
    conv_sample = jnp.swapaxes(xs[n_new:], 0, 1)[None]
    return (y_prompt, y_sample, k_prompt, v_prompt, kidx_prompt, h_prompt, conv_prompt,
            k_sample, v_sample, kidx_sample, h_sample, conv_sample)
```

```python
import functools

import numpy as np
import jax
import jax.numpy as jnp
from jax import lax
from jax.experimental import pallas as pl
from jax.experimental.pallas import tpu as pltpu

F32 = jnp.float32
BF16 = jnp.bfloat16
I32 = jnp.int32

D_MODEL = 2048
N_HEADS = 16
HEAD_DIM = 128
N_IDX_HEADS = 16
IDX_DIM = 64
N_LRU_BLOCKS = 16
LRU_BLOCK = 128
CONV_W = 4
LRU_C = 8.0
N_META = 16
TOPK = 256
PAGE = 128
NORM_EPS = 1e-6
NEG_BIG = -1e30
INT_MIN = -(2 ** 31)

ROW_BLOCK = 256
Q_BLOCK = 256
S_CHUNK = 256
BIG_CHUNK = 1024
HEADS_PER_STEP = 2
LOG2E = 1.4426950408889634
VMEM_LIMIT = 56 * 1024 * 1024


def _cparams(*sem):
    return pltpu.CompilerParams(dimension_semantics=sem, vmem_limit_bytes=VMEM_LIMIT)


def _rmsnorm_kernel(x_ref, g_ref, o_ref):
    x = x_ref[...]
    y = x * lax.rsqrt(jnp.mean(x * x, axis=-1, keepdims=True) + NORM_EPS)
    o_ref[...] = (y * g_ref[...]).astype(o_ref.dtype)


def _rmsnorm_bf16(x, g):
    m, d = x.shape
    return pl.pallas_call(
        _rmsnorm_kernel,
        grid=(m // ROW_BLOCK,),
        in_specs=[pl.BlockSpec((ROW_BLOCK, d), lambda i: (i, 0)),
                  pl.BlockSpec((1, d), lambda i: (0, 0))],
        out_specs=pl.BlockSpec((ROW_BLOCK, d), lambda i: (i, 0)),
        out_shape=jax.ShapeDtypeStruct((m, d), BF16),
        compiler_params=_cparams("parallel"),
        name="rmsnorm_in",
    )(x, g.reshape(1, d))


def _mm_kernel(x_ref, w_ref, *o_refs, scale):
    acc = jnp.dot(x_ref[...], w_ref[...], preferred_element_type=F32)
    if scale is not None:
        acc = acc * scale
    for o_ref in o_refs:
        o_ref[...] = acc.astype(o_ref.dtype)


def _matmul(x, w, out_dtypes, *, scale=None, tm=1280, tn=1024, name="matmul"):
    m, k = x.shape
    n = w.shape[1]
    tn = min(tn, n)
    assert m % tm == 0 and n % tn == 0
    single = not isinstance(out_dtypes, tuple)
    dts = (out_dtypes,) if single else out_dtypes
    outs = pl.pallas_call(
        functools.partial(_mm_kernel, scale=scale),
        grid=(n // tn, m // tm),
        in_specs=[pl.BlockSpec((tm, k), lambda j, i: (i, 0)),
                  pl.BlockSpec((k, tn), lambda j, i: (0, j))],
        out_specs=[pl.BlockSpec((tm, tn), lambda j, i: (i, j)) for _ in dts],
        out_shape=[jax.ShapeDtypeStruct((m, n), dt) for dt in dts],
        compiler_params=_cparams("parallel", "parallel"),
        name=name,
    )(x, w)
    return outs[0] if single else outs


def _softplus(x):
    return jnp.maximum(x, 0.0) + jnp.log1p(jnp.exp(-jnp.abs(x)))


def _sigmoid(x):
    return 1.0 / (1.0 + jnp.exp(-x))


def _lru_coeffs(xc, wa_ref, wx_ref, ba, bx, lam):
    rs, is_ = [], []
    for n in range(N_LRU_BLOCKS):
        sl = slice(n * LRU_BLOCK, (n + 1) * LRU_BLOCK)
        xb = xc[:, sl].astype(BF16)
        rs.append(jnp.dot(xb, wa_ref[n], preferred_element_type=F32))
        is_.append(jnp.dot(xb, wx_ref[n], preferred_element_type=F32))
    r = _sigmoid(jnp.concatenate(rs, axis=-1) + ba)
    i = _sigmoid(jnp.concatenate(is_, axis=-1) + bx)
    log_a = (-LRU_C * _softplus(-lam)) * r
    a = jnp.exp(log_a)
    u = jnp.sqrt(1.0 - a * a) * (i * xc)
    return a, u


def _lru_prompt_kernel(xr_ref, gr_ref, cw_ref, cb_ref, wa_ref, wx_ref, ba_ref, bx_ref, lam_ref,
                       hs_ref, y_ref, xp_scr, a_scr, u_scr, h_scr):
    tb = ROW_BLOCK

    @pl.when(pl.program_id(0) == 0)
    def _():
        xp_scr[0:8, :] = jnp.zeros((8, D_MODEL), F32)
        h_scr[...] = jnp.zeros((8, D_MODEL), F32)

    x = xr_ref[...]
    xp_scr[8:8 + tb, :] = x
    xc = cb_ref[...] + xp_scr[5:5 + tb, :] * cw_ref[0:1, :]
    xc = xc + xp_scr[6:6 + tb, :] * cw_ref[1:2, :]
    xc = xc + xp_scr[7:7 + tb, :] * cw_ref[2:3, :]
    xc = xc + x * cw_ref[3:4, :]
    xp_scr[0:8, :] = x[tb - 8:tb, :]

    a, u = _lru_coeffs(xc, wa_ref, wx_ref, ba_ref[...], bx_ref[...], lam_ref[...])
    a_scr[...] = a
    u_scr[...] = u

    def group(g, h):
        r0 = pl.multiple_of(g * 8, 8)
        a8 = a_scr[pl.ds(r0, 8), :]
        u8 = u_scr[pl.ds(r0, 8), :]
        rows = []
        for r in range(8):
            h = a8[r:r + 1, :] * h + u8[r:r + 1, :]
            rows.append(h)
        hs_ref[pl.ds(r0, 8), :] = jnp.concatenate(rows, axis=0)
        return h

    h_last = lax.fori_loop(0, tb // 8, group, h_scr[0:1, :])
    h_scr[0:1, :] = h_last

    g = gr_ref[...]
    y_ref[...] = (hs_ref[...] * (g * _sigmoid(g))).astype(y_ref.dtype)


def _lru_prompt(rg, cw, cb, wa, wx, ba, bx, lam, n_rows):
    d = D_MODEL
    row = lambda i: (i, 0)
    const2 = lambda i: (0, 0)
    const3 = lambda i: (0, 0, 0)
    return pl.pallas_call(
        _lru_prompt_kernel,
        grid=(n_rows // ROW_BLOCK,),
        in_specs=[pl.BlockSpec((ROW_BLOCK, d), row), pl.BlockSpec((ROW_BLOCK, d), lambda i: (i, 1)),
                  pl.BlockSpec((CONV_W, d), const2), pl.BlockSpec((1, d), const2),
                  pl.BlockSpec((N_LRU_BLOCKS, LRU_BLOCK, LRU_BLOCK), const3),
                  pl.BlockSpec((N_LRU_BLOCKS, LRU_BLOCK, LRU_BLOCK), const3),
                  pl.BlockSpec((1, d), const2), pl.BlockSpec((1, d), const2), pl.BlockSpec((1, d), const2)],
        out_specs=[pl.BlockSpec((ROW_BLOCK, d), row), pl.BlockSpec((ROW_BLOCK, d), row)],
        out_shape=[jax.ShapeDtypeStruct((n_rows, d), F32), jax.ShapeDtypeStruct((n_rows, d), BF16)],
        scratch_shapes=[pltpu.VMEM((8 + ROW_BLOCK, d), F32), pltpu.VMEM((ROW_BLOCK, d), F32),
                        pltpu.VMEM((ROW_BLOCK, d), F32), pltpu.VMEM((8, d), F32)],
        compiler_params=_cparams("arbitrary"),
        name="lru_prompt",
    )(rg, rg, cw, cb, wa, wx, ba, bx, lam)


def _lru_sample_kernel(xs_ref, gr_ref, h0_ref, cw_ref, cb_ref, wa_ref, wx_ref, ba_ref, bx_ref, lam_ref,
                       hs_ref, y_ref):
    n_t = hs_ref.shape[0]
    h = h0_ref[...]
    for t in range(n_t):
        xc = cb_ref[...] + xs_ref[t] * cw_ref[0:1, :]
        for j in range(1, CONV_W):
            xc = xc + xs_ref[t + j] * cw_ref[j:j + 1, :]
        a, u = _lru_coeffs(xc, wa_ref, wx_ref, ba_ref[...], bx_ref[...], lam_ref[...])
        h = a * h + u
        hs_ref[t] = h
        g = gr_ref[t]
        y_ref[t] = (h * (g * _sigmoid(g))).astype(y_ref.dtype)


def _lru_sample(xs, gr, h0, cw, cb, wa, wx, ba, bx, lam):
    n_t, b, d = gr.shape
    return pl.pallas_call(
        _lru_sample_kernel,
        out_shape=[jax.ShapeDtypeStruct((n_t, b, d), F32), jax.ShapeDtypeStruct((n_t, b, d), BF16)],
        compiler_params=pltpu.CompilerParams(vmem_limit_bytes=VMEM_LIMIT),
        name="lru_sample",
    )(xs, gr, h0, cw, cb, wa, wx, ba, bx, lam)


LOWEST_KEY = INT_MIN + 0x00800000


def _key_to_f32(key):
    return pltpu.bitcast(key ^ ((key >> 31) & 0x7FFFFFFF), F32)


def _bisect_threshold(count_ge, zero):
    thr = jnp.where(count_ge(jnp.zeros(zero.shape, F32)) >= TOPK, zero, zero + INT_MIN)

    def body(p, thr):
        cand = thr + jnp.left_shift(jnp.int32(1), 30 - p)
        return jnp.where(count_ge(_key_to_f32(cand)) >= TOPK, cand, thr)

    thr = lax.fori_loop(0, 31, body, thr)
    return _key_to_f32(jnp.maximum(thr, LOWEST_KEY))


def _for_key_rows(n_small, body, carry):
    per_big = BIG_CHUNK // S_CHUNK
    n_big = n_small // per_big
    carry = lax.fori_loop(
        0, n_big, lambda c, x: body(pl.multiple_of(c * BIG_CHUNK, BIG_CHUNK), BIG_CHUNK, x), carry)
    return lax.fori_loop(
        n_big * per_big, n_small, lambda c, x: body(pl.multiple_of(c * S_CHUNK, S_CHUNK), S_CHUNK, x), carry)


def _prompt_attn_kernel(qit_ref, wt_ref, ki_ref, qt_ref, k_ref, vt_ref, slope_ref, o_ref, sel_scr, s_scr):
    i = pl.program_id(0)
    h = pl.program_id(1)
    hps = HEADS_PER_STEP
    n_chunks = i + 1
    t0 = i * Q_BLOCK
    sc, qb = S_CHUNK, Q_BLOCK

    @pl.when(h == 0)
    def _select():
        def score_chunk(c, carry):
            r0 = pl.multiple_of(c * sc, sc)
            kc = ki_ref[pl.ds(r0, sc), :]
            acc = jnp.zeros((sc, qb), F32)
            for hh in range(N_IDX_HEADS):
                s = jnp.dot(kc, qit_ref[hh * IDX_DIM:(hh + 1) * IDX_DIM, :], preferred_element_type=F32)
                acc = acc + jnp.maximum(s, 0.0) * wt_ref[hh:hh + 1, :]
            row = r0 + lax.broadcasted_iota(I32, (sc, qb), 0)
            col = t0 + lax.broadcasted_iota(I32, (sc, qb), 1)
            sel_scr[pl.ds(r0, sc), :] = jnp.where(row <= col, acc, -jnp.inf)
            return carry

        lax.fori_loop(0, n_chunks, score_chunk, 0)

        def count_ge(cand):
            def body(r0, rows, cnt):
                ind = jnp.where(sel_scr[pl.ds(r0, rows), :] >= cand, 1.0, 0.0)
                return cnt + ind.reshape(rows // 8, 8, qb).sum(axis=0)

            cnt8 = _for_key_rows(n_chunks, body, jnp.zeros((8, qb), F32))
            return cnt8.sum(axis=0, keepdims=True)

        thr = _bisect_threshold(count_ge, jnp.zeros((1, qb), I32))

        def mark(r0, rows, carry):
            row = r0 + lax.broadcasted_iota(I32, (rows, qb), 0)
            col = t0 + lax.broadcasted_iota(I32, (rows, qb), 1)
            sel = sel_scr[pl.ds(r0, rows), :] >= thr
            sel_scr[pl.ds(r0, rows), :] = jnp.where(sel, (row - col).astype(F32), NEG_BIG)
            return carry

        _for_key_rows(n_chunks, mark, 0)

    qh = [qt_ref[u * HEAD_DIM:(u + 1) * HEAD_DIM, :] for u in range(hps)]
    slope = [slope_ref[u, 0:1, :] for u in range(hps)]

    def logits(r0, rows, mx):
        kk = k_ref[pl.ds(r0, rows), :]
        nd = sel_scr[pl.ds(r0, rows), :]
        out = []
        for u in range(hps):
            s = jnp.dot(kk[:, u * HEAD_DIM:(u + 1) * HEAD_DIM], qh[u], preferred_element_type=F32)
            s = s + nd * slope[u]
            s_scr[u, pl.ds(r0, rows), :] = s
            out.append(jnp.maximum(mx[u], s.reshape(rows // 8, 8, qb).max(axis=0)))
        return tuple(out)

    mx8 = _for_key_rows(n_chunks, logits, tuple(jnp.full((8, qb), -3e38, F32) for _ in range(hps)))
    m = [x.max(axis=0, keepdims=True) for x in mx8]

    def weigh(r0, rows, carry):
        c0 = r0 // sc
        out = []
        for u in range(hps):
            l8, acc = carry[u]
            p = jnp.exp2(s_scr[u, pl.ds(r0, rows), :] - m[u])
            l8 = l8 + p.reshape(rows // 8, 8, qb).sum(axis=0)
            v = jnp.concatenate([vt_ref[u, c0 + j] for j in range(rows // sc)], axis=1)
            acc = acc + jnp.dot(v, p.astype(BF16), preferred_element_type=F32)
            out.append((l8, acc))
        return tuple(out)

    zero = (jnp.zeros((8, qb), F32), jnp.zeros((HEAD_DIM, qb), F32))
    res = _for_key_rows(n_chunks, weigh, tuple(zero for _ in range(hps)))
    for u in range(hps):
        l8, acc = res[u]
        o_ref[u * HEAD_DIM:(u + 1) * HEAD_DIM, :] = acc / l8.sum(axis=0, keepdims=True)


def _prompt_attention(qit, wt, ki, qt, kv_bf, vt4, slopes, t_pad):
    n_blk = t_pad // Q_BLOCK
    hps = HEADS_PER_STEP
    return pl.pallas_call(
        _prompt_attn_kernel,
        grid=(n_blk, N_HEADS // hps),
        in_specs=[pl.BlockSpec((N_IDX_HEADS * IDX_DIM, Q_BLOCK), lambda i, h: (0, i)),
                  pl.BlockSpec((N_IDX_HEADS, Q_BLOCK), lambda i, h: (0, i)),
                  pl.BlockSpec((t_pad, IDX_DIM), lambda i, h: (0, 0)),
                  pl.BlockSpec((hps * HEAD_DIM, Q_BLOCK), lambda i, h: (h, i)),
                  pl.BlockSpec((t_pad, hps * HEAD_DIM), lambda i, h: (0, h)),
                  pl.BlockSpec((hps, t_pad // S_CHUNK, HEAD_DIM, S_CHUNK), lambda i, h: (h, 0, 0, 0)),
                  pl.BlockSpec((hps, 8, Q_BLOCK), lambda i, h: (h, 0, 0))],
        out_specs=pl.BlockSpec((hps * HEAD_DIM, Q_BLOCK), lambda i, h: (h, i)),
        out_shape=jax.ShapeDtypeStruct((N_HEADS * HEAD_DIM, t_pad), F32),
        scratch_shapes=[pltpu.VMEM((t_pad, Q_BLOCK), F32), pltpu.VMEM((hps, t_pad, Q_BLOCK), F32)],
        compiler_params=_cparams("arbitrary", "arbitrary"),
        name="prompt_attention",
    )(qit, wt, ki, qt, kv_bf, vt4, slopes)


N_SROWS = 4 * N_HEADS
PAGE_ROWS = PAGE * N_HEADS
PAGES_PER_STEP = 4


def _sample_score_kernel(pt_ref, *refs, n_pages):
    kidx_refs = refs[:n_pages]
    kinew_ref, qi_ref, w_ref, o_ref = refs[n_pages:]
    qi = qi_ref[...]
    w = w_ref[...]

    def page_scores(page):
        s = jnp.dot(qi, page.astype(BF16), preferred_element_type=F32)
        x = jnp.maximum(s, 0.0) * w
        x8 = x.reshape(N_IDX_HEADS // 2, 8, PAGE).sum(axis=0)
        return x8 + pltpu.roll(x8, 4, 0)

    for p in range(n_pages):
        o_ref[:, p * PAGE:(p + 1) * PAGE] = page_scores(kidx_refs[p][...])
    slot = lax.broadcasted_iota(I32, (8, PAGE), 1)
    q = lax.broadcasted_iota(I32, (8, PAGE), 0) % 4
    o_ref[:, n_pages * PAGE:] = jnp.where(slot <= q, page_scores(kinew_ref[...]), NEG_BIG)


def _sample_scores(page_table, cache_kidx_t, kinew_t, qi_s, w_s):
    b, n_pages = page_table.shape

    def page_index(s, pt, *, p):
        return (0, pt[s, p], 0, 0)

    grid_spec = pltpu.PrefetchScalarGridSpec(
        num_scalar_prefetch=1,
        grid=(b,),
        in_specs=[pl.BlockSpec((None, None, IDX_DIM, PAGE), functools.partial(page_index, p=p))
                  for p in range(n_pages)]
        + [pl.BlockSpec((None, IDX_DIM, PAGE), lambda s, pt: (s, 0, 0)),
           pl.BlockSpec((None, N_SROWS, IDX_DIM), lambda s, pt: (s, 0, 0)),
           pl.BlockSpec((None, N_SROWS, PAGE), lambda s, pt: (s, 0, 0))],
        out_specs=pl.BlockSpec((None, 8, (n_pages + 1) * PAGE), lambda s, pt: (s, 0, 0)),
    )
    return pl.pallas_call(
        functools.partial(_sample_score_kernel, n_pages=n_pages),
        grid_spec=grid_spec,
        out_shape=jax.ShapeDtypeStruct((b, 8, (n_pages + 1) * PAGE), F32),
        compiler_params=_cparams("arbitrary"),
        name="sample_scores",
    )(page_table, *([cache_kidx_t] * n_pages), kinew_t, qi_s, w_s)


def _sample_select_kernel(s_ref, e_ref, o_ref):
    sc = s_ref[...]
    rows, width = sc.shape

    def count_ge(cand):
        return jnp.sum(jnp.where(sc >= cand, 1.0, 0.0), axis=1, keepdims=True)

    thr = _bisect_threshold(count_ge, jnp.zeros((rows, 1), I32))
    sel = jnp.where(sc >= thr, 1.0, 0.0).astype(BF16)
    n_tiles = width // PAGE
    for t in range(n_tiles):
        x = jnp.dot(sel[:, t * PAGE:(t + 1) * PAGE], e_ref[...], preferred_element_type=F32)
        lo = t * PAGE_ROWS
        hi = min(lo + PAGE_ROWS, o_ref.shape[1])
        o_ref[:, lo:hi] = x[:, :hi - lo]


def _sample_select(scores, expand, out_width):
    rows, width = scores.shape
    rb = 64
    return pl.pallas_call(
        _sample_select_kernel,
        grid=(rows // rb,),
        in_specs=[pl.BlockSpec((rb, width), lambda i: (i, 0)),
                  pl.BlockSpec((PAGE, PAGE_ROWS), lambda i: (0, 0))],
        out_specs=pl.BlockSpec((rb, out_width), lambda i: (i, 0)),
        out_shape=jax.ShapeDtypeStruct((rows, out_width), F32),
        compiler_params=_cparams("parallel"),
        name="sample_select",
    )(scores, expand)


def _sample_attn_kernel(pt_ref, q_ref, *refs, n_steps, past_len):
    pps = PAGES_PER_STEP
    k_refs, v_refs = refs[:pps], refs[pps:2 * pps]
    kn_ref, vn_ref, selp_ref, seln_ref, slope_ref, hm_ref, o_ref, m_scr, l_scr, acc_scr = refs[2 * pps:]
    jj = pl.program_id(1)

    @pl.when(jj == 0)
    def _():
        m_scr[...] = jnp.full(m_scr.shape, -3e38, F32)
        l_scr[...] = jnp.zeros(l_scr.shape, F32)
        acc_scr[...] = jnp.zeros(acc_scr.shape, F32)

    def step(k_list, v_list, sel8, pos0):
        q = q_ref[...]
        s = jnp.concatenate(
            [lax.dot_general(q, k.astype(BF16), (((1,), (1,)), ((), ())), preferred_element_type=F32)
             for k in k_list], axis=1)
        width = s.shape[1]
        lane = lax.broadcasted_iota(I32, (8, width), 1)
        rq = lax.broadcasted_iota(I32, (8, width), 0) % 4
        dist = (pos0 - past_len + lane // N_HEADS - rq).astype(F32)
        hm = jnp.concatenate([hm_ref[...]] * (width // PAGE_ROWS), axis=1) if width > PAGE_ROWS \
            else hm_ref[:, :width]
        g = N_SROWS // 8
        x = s.reshape(g, 8, width) + slope_ref[...].reshape(g, 8, 1) * dist[None]
        x = jnp.where(sel8[None] > 0.5, x, NEG_BIG) + hm.reshape(g, 8, width)
        x = x.reshape(N_SROWS, width)
        m = m_scr[...]
        m_new = jnp.maximum(m, jnp.max(x, axis=1, keepdims=True))
        alpha = jnp.exp2(m - m_new)
        p = jnp.exp2(x - m_new)
        l_scr[...] = alpha * l_scr[...] + jnp.sum(p, axis=1, keepdims=True)
        pb = p.astype(BF16)
        pv = None
        off = 0
        for v in v_list:
            part = jnp.dot(pb[:, off:off + v.shape[0]], v.astype(BF16), preferred_element_type=F32)
            pv = part if pv is None else pv + part
            off += v.shape[0]
        acc_scr[...] = alpha * acc_scr[...] + pv
        m_scr[...] = m_new

    @pl.when(jj < n_steps)
    def _():
        step([r[...] for r in k_refs], [r[...] for r in v_refs], selp_ref[...], jj * (pps * PAGE))

    @pl.when(jj == n_steps)
    def _():
        step([kn_ref[...]], [vn_ref[...]], seln_ref[...], past_len)
        o_ref[...] = acc_scr[...] / l_scr[...]


def _sample_attention(page_table, q_s, cache_k2, cache_v2, knew, vnew, sel_exp, slope_rows, head_mask):
    b, n_pages = page_table.shape
    pps = PAGES_PER_STEP
    assert n_pages % pps == 0
    n_steps = n_pages // pps
    nw = knew.shape[1]

    def page_index(s, jj, pt, *, u):
        return (pt[s, jnp.minimum(jj, n_steps - 1) * pps + u], 0)

    page_specs = [pl.BlockSpec((PAGE_ROWS, HEAD_DIM), functools.partial(page_index, u=u)) for u in range(pps)]
    grid_spec = pltpu.PrefetchScalarGridSpec(
        num_scalar_prefetch=1,
        grid=(b, n_steps + 1),
        in_specs=[pl.BlockSpec((None, N_SROWS, HEAD_DIM), lambda s, jj, pt: (s, 0, 0))]
        + page_specs + page_specs
        + [pl.BlockSpec((None, nw, HEAD_DIM), lambda s, jj, pt: (s, 0, 0)),
           pl.BlockSpec((None, nw, HEAD_DIM), lambda s, jj, pt: (s, 0, 0)),
           pl.BlockSpec((None, 8, pps * PAGE_ROWS), lambda s, jj, pt: (s, 0, jnp.minimum(jj, n_steps - 1))),
           pl.BlockSpec((None, 8, nw), lambda s, jj, pt: (s, 0, n_pages * PAGE_ROWS // nw)),
           pl.BlockSpec((N_SROWS, 1), lambda s, jj, pt: (0, 0)),
           pl.BlockSpec((N_SROWS, PAGE_ROWS), lambda s, jj, pt: (0, 0))],
        out_specs=pl.BlockSpec((None, N_SROWS, HEAD_DIM), lambda s, jj, pt: (s, 0, 0)),
        scratch_shapes=[pltpu.VMEM((N_SROWS, 1), F32), pltpu.VMEM((N_SROWS, 1), F32),
                        pltpu.VMEM((N_SROWS, HEAD_DIM), F32)],
    )
    return pl.pallas_call(
        functools.partial(_sample_attn_kernel, n_steps=n_steps, past_len=n_pages * PAGE),
        grid_spec=grid_spec,
        out_shape=jax.ShapeDtypeStruct((b, N_SROWS, HEAD_DIM), F32),
        compiler_params=_cparams("arbitrary", "arbitrary"),
        name="sample_attention",
    )(page_table, q_s, *([cache_k2] * pps), *([cache_v2] * pps), knew, vnew, sel_exp, sel_exp, slope_rows,
      head_mask)


def _merge_kernel(ot_ref, os_ref, ga_ref, w_ref, pa_ref, g1_ref, g2_ref, out_ref, *, n_prompt_blocks):
    o = jnp.where(pl.program_id(0) < n_prompt_blocks, ot_ref[...].T, os_ref[...])
    ga = ga_ref[...]
    og = (o * (ga * _sigmoid(ga))).astype(BF16)
    pb = jnp.dot(og, w_ref[...], preferred_element_type=F32)
    merged = _sigmoid(g1_ref[...]) * pa_ref[...] + _sigmoid(g2_ref[...]) * pb
    out_ref[...] = merged.astype(out_ref.dtype)


def _merge(ot, o_s, ga, w_pb, pa, gg):
    m, d = ga.shape
    npb = ot.shape[1] // ROW_BLOCK
    row = lambda i: (i, 0)
    return pl.pallas_call(
        functools.partial(_merge_kernel, n_prompt_blocks=npb),
        grid=(m // ROW_BLOCK,),
        in_specs=[pl.BlockSpec((d, ROW_BLOCK), lambda i: (0, jnp.minimum(i, npb - 1))),
                  pl.BlockSpec((ROW_BLOCK, d), lambda i: (jnp.maximum(i - npb, 0), 0)),
                  pl.BlockSpec((ROW_BLOCK, d), row),
                  pl.BlockSpec((d, d), lambda i: (0, 0)),
                  pl.BlockSpec((ROW_BLOCK, d), row), pl.BlockSpec((ROW_BLOCK, d), row),
                  pl.BlockSpec((ROW_BLOCK, d), lambda i: (i, 1))],
        out_specs=pl.BlockSpec((ROW_BLOCK, d), row),
        out_shape=jax.ShapeDtypeStruct((m, d), BF16),
        compiler_params=_cparams("parallel"),
        name="merge",
    )(ot, o_s, ga, w_pb, pa, gg, gg)


def _out_kernel(mg_ref, w_ref, x_ref, g_ref, y_ref):
    res = x_ref[...] + jnp.dot(mg_ref[...], w_ref[...], preferred_element_type=F32)
    y = res * lax.rsqrt(jnp.mean(res * res, axis=-1, keepdims=True) + NORM_EPS)
    y_ref[...] = y * g_ref[...]


def _out_proj(merged, w_out, x, g):
    m, d = x.shape
    row = lambda i: (i, 0)
    return pl.pallas_call(
        _out_kernel,
        grid=(m // ROW_BLOCK,),
        in_specs=[pl.BlockSpec((ROW_BLOCK, d), row), pl.BlockSpec((d, d), lambda i: (0, 0)),
                  pl.BlockSpec((ROW_BLOCK, d), row), pl.BlockSpec((1, d), lambda i: (0, 0))],
        out_specs=pl.BlockSpec((ROW_BLOCK, d), row),
        out_shape=jax.ShapeDtypeStruct((m, d), F32),
        compiler_params=_cparams("parallel"),
        name="out_proj",
    )(merged, w_out, x, g.reshape(1, d))


def kernel(x_prompt, x_sample, cache_k, cache_v, cache_kidx, state_h, state_conv, page_table, meta_tokens,
           norm_g, w_in, conv_w, conv_b, lru_wa, lru_ba, lru_wx, lru_bx, lru_lambda, w_proj_a, w_proj_b,
           w_out, final_g):
    assert x_prompt.shape[0] == 1 and norm_g.shape[0] == 1
    d = D_MODEL
    seq = x_prompt.shape[1]
    t_p = seq + N_META
    t_pad = -(-t_p // Q_BLOCK) * Q_BLOCK
    n_seq, n_new = x_sample.shape[:2]
    n_s = n_seq * n_new
    n_pages = page_table.shape[1]
    past_len = n_pages * PAGE
    assert n_new == 4 and n_s % ROW_BLOCK == 0
    m_all = t_pad + n_s

    x_all = jnp.concatenate([meta_tokens.astype(F32), x_prompt[0], jnp.zeros((t_pad - t_p, d), F32),
                             x_sample.reshape(n_s, d)], axis=0)
    xn = _rmsnorm_bf16(x_all, norm_g[0])

    w = w_in[0]
    cuts = np.cumsum([d, d, d, d, d, d, N_IDX_HEADS * IDX_DIM, IDX_DIM, N_IDX_HEADS, d, d]).tolist()
    wb = lambda lo, hi: w[:, lo:hi].astype(BF16)
    rg = _matmul(xn, wb(0, cuts[1]), F32, name="proj_rnn")
    qv = _matmul(xn, wb(cuts[1], cuts[2]), BF16, scale=LOG2E * HEAD_DIM ** -0.5, name="proj_q")
    kv, kv_bf = _matmul(xn, wb(cuts[2], cuts[4]), (F32, BF16), name="proj_kv")
    ga = _matmul(xn, wb(cuts[4], cuts[5]), F32, name="proj_ga")
    qi = _matmul(xn, wb(cuts[5], cuts[6]), BF16, scale=IDX_DIM ** -0.5, name="proj_qi")
    w_kw = jnp.pad(w[:, cuts[6]:cuts[8]], ((0, 0), (0, 128 - (cuts[8] - cuts[6])))).astype(BF16)
    kw = _matmul(xn, w_kw, F32, name="proj_kiwi")
    gg = _matmul(xn, wb(cuts[8], cuts[10]), F32, name="proj_gates")

    xr, gr = rg[:, :d], rg[:, d:]
    k_all, v_all = kv[:, :d], kv[:, d:]
    ki_all = kw[:, :IDX_DIM]
    wi_all = kw[:, IDX_DIM:IDX_DIM + N_IDX_HEADS] * (N_IDX_HEADS ** -0.5)

    cw, cb = conv_w[0], conv_b[0].reshape(1, d)
    wa, wx = lru_wa[0].astype(BF16), lru_wx[0].astype(BF16)
    ba, bx, lam = lru_ba[0].reshape(1, d), lru_bx[0].reshape(1, d), lru_lambda[0].reshape(1, d)
    hs_p, y_p = _lru_prompt(rg, cw, cb, wa, wx, ba, bx, lam, t_pad)
    xr_s = jnp.swapaxes(xr[t_pad:].reshape(n_seq, n_new, d), 0, 1)
    gr_s = jnp.swapaxes(gr[t_pad:].reshape(n_seq, n_new, d), 0, 1)
    xs = jnp.concatenate([jnp.swapaxes(state_conv[0], 0, 1), xr_s], axis=0)
    hs_s, y_s = _lru_sample(xs, gr_s, state_h[0], cw, cb, wa, wx, ba, bx, lam)
    y_rnn = jnp.concatenate([y_p, jnp.swapaxes(y_s, 0, 1).reshape(n_s, d)], axis=0)

    slopes = jnp.asarray(LOG2E * 2.0 ** (-8.0 * np.arange(1, N_HEADS + 1) / N_HEADS), dtype=F32)
    qit = qi[:t_pad].T
    wt = wi_all[:t_pad].T
    qt = qv[:t_pad].T
    vt4 = kv_bf[:t_pad, d:].reshape(t_pad // S_CHUNK, S_CHUNK, N_HEADS, HEAD_DIM).transpose(2, 0, 3, 1)
    slope_b = jnp.broadcast_to(slopes[:, None, None], (N_HEADS, 8, Q_BLOCK))
    ot = _prompt_attention(qit, wt, ki_all[:t_pad].astype(BF16), qt, kv_bf, vt4, slope_b, t_pad)

    hq = lambda a, width: a.reshape(n_seq, n_new, N_HEADS, width).transpose(0, 2, 1, 3).reshape(
        n_seq, N_HEADS * n_new, width)
    qi_s = hq(qi[t_pad:], IDX_DIM)
    w_s = jnp.broadcast_to(
        wi_all[t_pad:].reshape(n_seq, n_new, N_IDX_HEADS).transpose(0, 2, 1).reshape(n_seq, -1, 1),
        (n_seq, N_IDX_HEADS * n_new, PAGE))
    kinew_t = jnp.pad(jnp.swapaxes(ki_all[t_pad:].reshape(n_seq, n_new, IDX_DIM), 1, 2),
                      ((0, 0), (0, 0), (0, PAGE - n_new)))
    scores = _sample_scores(page_table, jnp.swapaxes(cache_kidx, 2, 3), kinew_t, qi_s, w_s)
    col = np.arange(PAGE_ROWS)
    expand = jnp.asarray(col[None, :] // N_HEADS == np.arange(PAGE)[:, None], dtype=BF16)
    head_mask = jnp.asarray(np.where(np.arange(N_SROWS)[:, None] // n_new == col[None, :] % N_HEADS,
                                     0.0, 4.0 * NEG_BIG), dtype=F32)
    sel_exp = _sample_select(scores.reshape(n_seq * 8, past_len + PAGE), expand, (past_len + 8) * N_HEADS)
    sel_exp = sel_exp.reshape(n_seq, 8, (past_len + 8) * N_HEADS)
    new_rows = lambda a: jnp.pad(a.reshape(n_seq, n_new * N_HEADS, HEAD_DIM),
                                 ((0, 0), (0, (8 - n_new) * N_HEADS), (0, 0)))
    slope_rows = jnp.repeat(slopes, n_new).reshape(N_SROWS, 1)
    o_s = _sample_attention(page_table, hq(qv[t_pad:], HEAD_DIM),
                            cache_k.reshape(-1, HEAD_DIM), cache_v.reshape(-1, HEAD_DIM),
                            new_rows(k_all[t_pad:]), new_rows(v_all[t_pad:]), sel_exp, slope_rows, head_mask)
    o_s = o_s.reshape(n_seq, N_HEADS, n_new, HEAD_DIM).transpose(0, 2, 1, 3).reshape(n_s, d)

    pa = _matmul(y_rnn, w_proj_a[0].astype(BF16), F32, name="proj_a")
    merged = _merge(ot, o_s, ga, w_proj_b[0].astype(BF16), pa, gg)
    y_all = _out_proj(merged, w_out[0].astype(BF16), x_all, final_g)

    y_prompt = y_all[N_META:t_p][None]
    y_sample = y_all[t_pad:].reshape(n_seq, n_new, d)
    heads = lambda a, *lead: a.reshape(*lead, N_HEADS, HEAD_DIM)
    k_prompt = heads(k_all[:t_p], 1, 1, t_p)
    v_prompt = heads(v_all[:t_p], 1, 1, t_p)
    kidx_prompt = ki_all[:t_p][None, None]
    h_prompt = hs_p[t_p - 1][None, None]
    conv_prompt = xr[t_p - (CONV_W - 1):t_p][None, None]
    k_sample = heads(k_all[t_pad:], 1, n_seq, n_new)
    v_sample = heads(v_all[t_pad:], 1, n_seq, n_new)
    kidx_sample = ki_all[t_pad:].reshape(1, n_seq, n_new, IDX_DIM)
    h_sample = hs_s[n_new - 1][None]
    conv_sample = jnp.swapaxes(xs[n_new:], 0, 1)[None]
    return (y_prompt, y_sample, k_prompt, v_prompt, kidx_prompt, h_prompt, conv_prompt,
            k_sample, v_sample, kidx_sample, h_sample, conv_sample)
```

```python
import functools

import numpy as np
import jax
import jax.numpy as jnp
from jax import lax
from jax.experimental import pallas as pl
from jax.experimental.pallas import tpu as pltpu

F32 = jnp.float32
BF16 = jnp.bfloat16
I32 = jnp.int32

D_MODEL = 2048
N_HEADS = 16
HEAD_DIM = 128
N_IDX_HEADS = 16
IDX_DIM = 64
N_LRU_BLOCKS = 16
LRU_BLOCK = 128
CONV_W = 4
LRU_C = 8.0
N_META = 16
TOPK = 256
PAGE = 128
NORM_EPS = 1e-6
NEG_BIG = -1e30
INT_MIN = -(2 ** 31)

ROW_BLOCK = 256
Q_BLOCK = 256
S_CHUNK = 256
KEY_CHUNKS = (2048, 1024, S_CHUNK)
HEADS_PER_STEP = 2
LOG2E = 1.4426950408889634
VMEM_LIMIT = 56 * 1024 * 1024


def _cparams(*sem):
    return pltpu.CompilerParams(dimension_semantics=sem, vmem_limit_bytes=VMEM_LIMIT)


def _rmsnorm_kernel(x_ref, g_ref, o_ref):
    x = x_ref[...]
    y = x * lax.rsqrt(jnp.mean(x * x, axis=-1, keepdims=True) + NORM_EPS)
    o_ref[...] = (y * g_ref[...]).astype(o_ref.dtype)


def _rmsnorm_bf16(x, g):
    m, d = x.shape
    return pl.pallas_call(
        _rmsnorm_kernel,
        grid=(m // ROW_BLOCK,),
        in_specs=[pl.BlockSpec((ROW_BLOCK, d), lambda i: (i, 0)),
                  pl.BlockSpec((1, d), lambda i: (0, 0))],
        out_specs=pl.BlockSpec((ROW_BLOCK, d), lambda i: (i, 0)),
        out_shape=jax.ShapeDtypeStruct((m, d), BF16),
        compiler_params=_cparams("parallel"),
        name="rmsnorm_in",
    )(x, g.reshape(1, d))


def _mm_kernel(x_ref, w_ref, *o_refs, scale):
    acc = jnp.dot(x_ref[...], w_ref[...], preferred_element_type=F32)
    if scale is not None:
        acc = acc * scale
    for o_ref in o_refs:
        o_ref[...] = acc.astype(o_ref.dtype)


def _matmul(x, w, out_dtypes, *, cols=None, scale=None, tm=1280, tn=1024, name="matmul"):
    m, k = x.shape
    lo, hi = (0, w.shape[1]) if cols is None else cols
    n = hi - lo
    tn = min(tn, n)
    assert m % tm == 0 and n % tn == 0 and lo % tn == 0
    j0 = lo // tn
    single = not isinstance(out_dtypes, tuple)
    dts = (out_dtypes,) if single else out_dtypes
    outs = pl.pallas_call(
        functools.partial(_mm_kernel, scale=scale),
        grid=(n // tn, m // tm),
        in_specs=[pl.BlockSpec((tm, k), lambda j, i: (i, 0)),
                  pl.BlockSpec((k, tn), lambda j, i: (0, j0 + j))],
        out_specs=[pl.BlockSpec((tm, tn), lambda j, i: (i, j)) for _ in dts],
        out_shape=[jax.ShapeDtypeStruct((m, n), dt) for dt in dts],
        compiler_params=_cparams("parallel", "parallel"),
        name=name,
    )(x, w)
    return outs[0] if single else outs


def _softplus(x):
    return jnp.maximum(x, 0.0) + jnp.log1p(jnp.exp(-jnp.abs(x)))


def _sigmoid(x):
    return 0.5 * jnp.tanh(0.5 * x) + 0.5


def _lru_coeffs(xc, wa_ref, wx_ref, ba, bx, lam):
    rs, is_ = [], []
    for n in range(N_LRU_BLOCKS):
        sl = slice(n * LRU_BLOCK, (n + 1) * LRU_BLOCK)
        xb = xc[:, sl].astype(BF16)
        rs.append(jnp.dot(xb, wa_ref[n], preferred_element_type=F32))
        is_.append(jnp.dot(xb, wx_ref[n], preferred_element_type=F32))
    r = _sigmoid(jnp.concatenate(rs, axis=-1) + ba)
    i = _sigmoid(jnp.concatenate(is_, axis=-1) + bx)
    log_a = (-LRU_C * _softplus(-lam)) * r
    a = jnp.exp(log_a)
    u = jnp.sqrt(1.0 - a * a) * (i * xc)
    return a, u


def _lru_prompt_kernel(xr_ref, gr_ref, cw_ref, cb_ref, wa_ref, wx_ref, ba_ref, bx_ref, lam_ref,
                       hs_ref, y_ref, xp_scr, a_scr, u_scr, h_scr):
    tb = ROW_BLOCK

    @pl.when(pl.program_id(0) == 0)
    def _():
        xp_scr[0:8, :] = jnp.zeros((8, D_MODEL), F32)
        h_scr[...] = jnp.zeros((8, D_MODEL), F32)

    x = xr_ref[...]
    xp_scr[8:8 + tb, :] = x
    xc = cb_ref[...] + xp_scr[5:5 + tb, :] * cw_ref[0:1, :]
    xc = xc + xp_scr[6:6 + tb, :] * cw_ref[1:2, :]
    xc = xc + xp_scr[7:7 + tb, :] * cw_ref[2:3, :]
    xc = xc + x * cw_ref[3:4, :]
    xp_scr[0:8, :] = x[tb - 8:tb, :]

    a, u = _lru_coeffs(xc, wa_ref, wx_ref, ba_ref[...], bx_ref[...], lam_ref[...])
    a_scr[...] = a
    u_scr[...] = u

    def group(g, h):
        r0 = pl.multiple_of(g * 8, 8)
        a8 = a_scr[pl.ds(r0, 8), :]
        u8 = u_scr[pl.ds(r0, 8), :]
        rows = []
        for r in range(8):
            h = a8[r:r + 1, :] * h + u8[r:r + 1, :]
            rows.append(h)
        hs_ref[pl.ds(r0, 8), :] = jnp.concatenate(rows, axis=0)
        return h

    h_last = lax.fori_loop(0, tb // 8, group, h_scr[0:1, :])
    h_scr[0:1, :] = h_last

    g = gr_ref[...]
    y_ref[...] = (hs_ref[...] * (g * _sigmoid(g))).astype(y_ref.dtype)


def _lru_prompt(rg, cw, cb, wa, wx, ba, bx, lam, n_rows):
    d = D_MODEL
    row = lambda i: (i, 0)
    const2 = lambda i: (0, 0)
    const3 = lambda i: (0, 0, 0)
    return pl.pallas_call(
        _lru_prompt_kernel,
        grid=(n_rows // ROW_BLOCK,),
        in_specs=[pl.BlockSpec((ROW_BLOCK, d), row), pl.BlockSpec((ROW_BLOCK, d), lambda i: (i, 1)),
                  pl.BlockSpec((CONV_W, d), const2), pl.BlockSpec((1, d), const2),
                  pl.BlockSpec((N_LRU_BLOCKS, LRU_BLOCK, LRU_BLOCK), const3),
                  pl.BlockSpec((N_LRU_BLOCKS, LRU_BLOCK, LRU_BLOCK), const3),
                  pl.BlockSpec((1, d), const2), pl.BlockSpec((1, d), const2), pl.BlockSpec((1, d), const2)],
        out_specs=[pl.BlockSpec((ROW_BLOCK, d), row), pl.BlockSpec((ROW_BLOCK, d), row)],
        out_shape=[jax.ShapeDtypeStruct((n_rows, d), F32), jax.ShapeDtypeStruct((n_rows, d), BF16)],
        scratch_shapes=[pltpu.VMEM((8 + ROW_BLOCK, d), F32), pltpu.VMEM((ROW_BLOCK, d), F32),
                        pltpu.VMEM((ROW_BLOCK, d), F32), pltpu.VMEM((8, d), F32)],
        compiler_params=_cparams("arbitrary"),
        name="lru_prompt",
    )(rg, rg, cw, cb, wa, wx, ba, bx, lam)


def _lru_sample_kernel(xs_ref, gr_ref, h0_ref, cw_ref, cb_ref, wa_ref, wx_ref, ba_ref, bx_ref, lam_ref,
                       hs_ref, y_ref):
    n_t = hs_ref.shape[0]
    h = h0_ref[...]
    for t in range(n_t):
        xc = cb_ref[...] + xs_ref[t] * cw_ref[0:1, :]
        for j in range(1, CONV_W):
            xc = xc + xs_ref[t + j] * cw_ref[j:j + 1, :]
        a, u = _lru_coeffs(xc, wa_ref, wx_ref, ba_ref[...], bx_ref[...], lam_ref[...])
        h = a * h + u
        hs_ref[t] = h
        g = gr_ref[t]
        y_ref[t] = (h * (g * _sigmoid(g))).astype(y_ref.dtype)


def _lru_sample(xs, gr, h0, cw, cb, wa, wx, ba, bx, lam):
    n_t, b, d = gr.shape
    return pl.pallas_call(
        _lru_sample_kernel,
        out_shape=[jax.ShapeDtypeStruct((n_t, b, d), F32), jax.ShapeDtypeStruct((n_t, b, d), BF16)],
        compiler_params=pltpu.CompilerParams(vmem_limit_bytes=VMEM_LIMIT),
        name="lru_sample",
    )(xs, gr, h0, cw, cb, wa, wx, ba, bx, lam)


LOWEST_KEY = INT_MIN + 0x00800000
TIE_ROW_BITS = 14
TIE_ROW_LIMIT = 2 ** TIE_ROW_BITS - 1


def _key_to_f32(key):
    return pltpu.bitcast(key ^ ((key >> 31) & 0x7FFFFFFF), F32)


def _bisect_threshold(count_ge, zero):
    thr = jnp.where(count_ge(jnp.zeros(zero.shape, F32)) >= TOPK, zero, zero + INT_MIN)

    def body(p, thr):
        cand = thr + jnp.left_shift(jnp.int32(1), 30 - p)
        return jnp.where(count_ge(_key_to_f32(cand)) >= TOPK, cand, thr)

    thr = lax.fori_loop(0, 31, body, thr)
    return _key_to_f32(jnp.maximum(thr, LOWEST_KEY))


def _for_key_rows(n_small, body, carry):
    done = 0
    for size in KEY_CHUNKS:
        n_this = (n_small * S_CHUNK - done) // size
        carry = lax.fori_loop(
            0, n_this, lambda c, x, d=done, sz=size: body(pl.multiple_of(d + c * sz, sz), sz, x), carry)
        done = done + n_this * size
    return carry


def _prompt_attn_kernel(qit_ref, wt_ref, ki_ref, qt_ref, k_ref, vt_ref, slope_ref, o_ref, sel_scr, s_scr,
                        tie_scr):
    i = pl.program_id(0)
    h = pl.program_id(1)
    hps = HEADS_PER_STEP
    n_chunks = i + 1
    t0 = i * Q_BLOCK
    sc, qb = S_CHUNK, Q_BLOCK

    @pl.when(h == 0)
    def _select():
        def score_chunk(c, carry):
            r0 = pl.multiple_of(c * sc, sc)
            kc = ki_ref[pl.ds(r0, sc), :]
            acc = jnp.zeros((sc, qb), F32)
            for hh in range(N_IDX_HEADS):
                s = jnp.dot(kc, qit_ref[hh * IDX_DIM:(hh + 1) * IDX_DIM, :], preferred_element_type=F32)
                acc = acc + jnp.maximum(s, 0.0) * wt_ref[hh:hh + 1, :]
            row = r0 + lax.broadcasted_iota(I32, (sc, qb), 0)
            col = t0 + lax.broadcasted_iota(I32, (sc, qb), 1)
            sel_scr[pl.ds(r0, sc), :] = jnp.where(row <= col, acc, -jnp.inf)
            return carry

        lax.fori_loop(0, n_chunks, score_chunk, 0)

        def count_ge(cand):
            def body(r0, rows, cnt):
                ind = jnp.where(sel_scr[pl.ds(r0, rows), :] >= cand, 1.0, 0.0)
                return cnt + ind.reshape(rows // 8, 8, qb).sum(axis=0)

            cnt8 = _for_key_rows(n_chunks, body, jnp.zeros((8, qb), F32))
            return cnt8.sum(axis=0, keepdims=True)

        thr = _bisect_threshold(count_ge, jnp.zeros((1, qb), I32))

        def count_rows(indicator):
            def body(r0, rows, cnt):
                row = r0 + lax.broadcasted_iota(I32, (rows, qb), 0)
                ind = indicator(sel_scr[pl.ds(r0, rows), :], row)
                return cnt + ind.reshape(rows // 8, 8, qb).sum(axis=0)

            return _for_key_rows(n_chunks, body, jnp.zeros((8, qb), F32)).sum(axis=0, keepdims=True)

        need = TOPK - count_rows(lambda x, row: jnp.where(x > thr, 1.0, 0.0))
        tie_scr[...] = jnp.full(tie_scr.shape, TIE_ROW_LIMIT, I32)

        @pl.when(jnp.max(count_ge(thr)) > TOPK)
        def _():
            def body(p, last):
                cand = last + jnp.left_shift(jnp.int32(1), TIE_ROW_BITS - 1 - p)
                below = count_rows(lambda x, row: jnp.where(x == thr, jnp.where(row < cand, 1.0, 0.0), 0.0))
                return jnp.where(below < need, cand, last)

            tie_scr[0:1, :] = lax.fori_loop(0, TIE_ROW_BITS, body, jnp.zeros((1, qb), I32))

        last = tie_scr[0:1, :]

        def mark(r0, rows, carry):
            row = r0 + lax.broadcasted_iota(I32, (rows, qb), 0)
            col = t0 + lax.broadcasted_iota(I32, (rows, qb), 1)
            x = sel_scr[pl.ds(r0, rows), :]
            dist = (row - col).astype(F32)
            sel_scr[pl.ds(r0, rows), :] = jnp.where(
                x == thr, jnp.where(row <= last, dist, NEG_BIG), jnp.where(x > thr, dist, NEG_BIG))
            return carry

        _for_key_rows(n_chunks, mark, 0)

    qh = [qt_ref[u * HEAD_DIM:(u + 1) * HEAD_DIM, :] for u in range(hps)]
    slope = [slope_ref[u, 0:1, :] for u in range(hps)]

    def logits(r0, rows, mx):
        kk = k_ref[pl.ds(r0, rows), :]
        nd = sel_scr[pl.ds(r0, rows), :]
        out = []
        for u in range(hps):
            s = jnp.dot(kk[:, u * HEAD_DIM:(u + 1) * HEAD_DIM], qh[u], preferred_element_type=F32)
            s = s + nd * slope[u]
            s_scr[u, pl.ds(r0, rows), :] = s
            out.append(jnp.maximum(mx[u], s.reshape(rows // 8, 8, qb).max(axis=0)))
        return tuple(out)

    mx8 = _for_key_rows(n_chunks, logits, tuple(jnp.full((8, qb), -3e38, F32) for _ in range(hps)))
    m = [x.max(axis=0, keepdims=True) for x in mx8]

    def weigh(r0, rows, carry):
        c0 = r0 // sc
        out = []
        for u in range(hps):
            l8, acc = carry[u]
            p = jnp.exp2(s_scr[u, pl.ds(r0, rows), :] - m[u])
            l8 = l8 + p.reshape(rows // 8, 8, qb).sum(axis=0)
            v = jnp.concatenate([vt_ref[u, c0 + j] for j in range(rows // sc)], axis=1)
            acc = acc + jnp.dot(v, p.astype(BF16), preferred_element_type=F32)
            out.append((l8, acc))
        return tuple(out)

    zero = (jnp.zeros((8, qb), F32), jnp.zeros((HEAD_DIM, qb), F32))
    res = _for_key_rows(n_chunks, weigh, tuple(zero for _ in range(hps)))
    for u in range(hps):
        l8, acc = res[u]
        o_ref[u * HEAD_DIM:(u + 1) * HEAD_DIM, :] = acc / l8.sum(axis=0, keepdims=True)


def _prompt_attention(qit, wt, ki, qt, kv_bf, vt4, slopes, t_pad):
    n_blk = t_pad // Q_BLOCK
    hps = HEADS_PER_STEP
    assert t_pad <= TIE_ROW_LIMIT and t_pad % KEY_CHUNKS[-1] == 0
    return pl.pallas_call(
        _prompt_attn_kernel,
        grid=(n_blk, N_HEADS // hps),
        in_specs=[pl.BlockSpec((N_IDX_HEADS * IDX_DIM, Q_BLOCK), lambda i, h: (0, i)),
                  pl.BlockSpec((N_IDX_HEADS, Q_BLOCK), lambda i, h: (0, i)),
                  pl.BlockSpec((t_pad, IDX_DIM), lambda i, h: (0, 0)),
                  pl.BlockSpec((hps * HEAD_DIM, Q_BLOCK), lambda i, h: (h, i)),
                  pl.BlockSpec((t_pad, hps * HEAD_DIM), lambda i, h: (0, h)),
                  pl.BlockSpec((hps, t_pad // S_CHUNK, HEAD_DIM, S_CHUNK), lambda i, h: (h, 0, 0, 0)),
                  pl.BlockSpec((hps, 8, Q_BLOCK), lambda i, h: (h, 0, 0))],
        out_specs=pl.BlockSpec((hps * HEAD_DIM, Q_BLOCK), lambda i, h: (h, i)),
        out_shape=jax.ShapeDtypeStruct((N_HEADS * HEAD_DIM, t_pad), F32),
        scratch_shapes=[pltpu.VMEM((t_pad, Q_BLOCK), F32), pltpu.VMEM((hps, t_pad, Q_BLOCK), F32),
                        pltpu.VMEM((8, Q_BLOCK), I32)],
        compiler_params=_cparams("arbitrary", "arbitrary"),
        name="prompt_attention",
    )(qit, wt, ki, qt, kv_bf, vt4, slopes)


N_SROWS = 4 * N_HEADS
PAGE_ROWS = PAGE * N_HEADS
PAGES_PER_STEP = 8
PAGES_PER_GROUP = 4


def _sample_score_kernel(pt_ref, *refs, n_pages):
    kidx_refs = refs[:n_pages]
    kinew_ref, qi_ref, w_ref, o_ref = refs[n_pages:]
    qi = qi_ref[...]
    w = w_ref[...]

    def page_scores(page):
        s = jnp.dot(qi, page.astype(BF16), preferred_element_type=F32)
        x = jnp.maximum(s, 0.0) * w
        x8 = x.reshape(N_IDX_HEADS // 2, 8, PAGE).sum(axis=0)
        return x8 + pltpu.roll(x8, 4, 0)

    for p in range(n_pages):
        o_ref[:, p * PAGE:(p + 1) * PAGE] = page_scores(kidx_refs[p][...])
    slot = lax.broadcasted_iota(I32, (8, PAGE), 1)
    q = lax.broadcasted_iota(I32, (8, PAGE), 0) % 4
    o_ref[:, n_pages * PAGE:] = jnp.where(slot <= q, page_scores(kinew_ref[...]), NEG_BIG)


def _sample_scores(page_table, cache_kidx_t, kinew_t, qi_s, w_s):
    b, n_pages = page_table.shape

    def page_index(s, pt, *, p):
        return (0, pt[s, p], 0, 0)

    grid_spec = pltpu.PrefetchScalarGridSpec(
        num_scalar_prefetch=1,
        grid=(b,),
        in_specs=[pl.BlockSpec((None, None, IDX_DIM, PAGE), functools.partial(page_index, p=p))
                  for p in range(n_pages)]
        + [pl.BlockSpec((None, IDX_DIM, PAGE), lambda s, pt: (s, 0, 0)),
           pl.BlockSpec((None, N_SROWS, IDX_DIM), lambda s, pt: (s, 0, 0)),
           pl.BlockSpec((None, N_SROWS, PAGE), lambda s, pt: (s, 0, 0))],
        out_specs=pl.BlockSpec((None, 8, (n_pages + 1) * PAGE), lambda s, pt: (s, 0, 0)),
    )
    return pl.pallas_call(
        functools.partial(_sample_score_kernel, n_pages=n_pages),
        grid_spec=grid_spec,
        out_shape=jax.ShapeDtypeStruct((b, 8, (n_pages + 1) * PAGE), F32),
        compiler_params=_cparams("arbitrary"),
        name="sample_scores",
    )(page_table, *([cache_kidx_t] * n_pages), kinew_t, qi_s, w_s)


def _sample_select_kernel(s_ref, e_ref, o_ref):
    sc = s_ref[...]
    rows, width = sc.shape

    def count_ge(cand):
        return jnp.sum(jnp.where(sc >= cand, 1.0, 0.0), axis=1, keepdims=True)

    thr = _bisect_threshold(count_ge, jnp.zeros((rows, 1), I32))
    pos = lax.broadcasted_iota(I32, (rows, width), 1)
    need = TOPK - jnp.sum(jnp.where(sc > thr, 1.0, 0.0), axis=1, keepdims=True)

    def tie_body(p, last):
        cand = last + jnp.left_shift(jnp.int32(1), TIE_ROW_BITS - 1 - p)
        below = jnp.sum(jnp.where(sc == thr, jnp.where(pos < cand, 1.0, 0.0), 0.0), axis=1, keepdims=True)
        return jnp.where(below < need, cand, last)

    last = lax.fori_loop(0, TIE_ROW_BITS, tie_body, jnp.zeros((rows, 1), I32))
    sel = jnp.where(sc == thr, jnp.where(pos <= last, 1.0, 0.0), jnp.where(sc > thr, 1.0, 0.0)).astype(BF16)
    n_tiles = width // PAGE
    for t in range(n_tiles):
        x = jnp.dot(sel[:, t * PAGE:(t + 1) * PAGE], e_ref[...], preferred_element_type=F32)
        lo = t * PAGE_ROWS
        hi = min(lo + PAGE_ROWS, o_ref.shape[1])
        o_ref[:, lo:hi] = x[:, :hi - lo]


def _sample_select(scores, expand, out_width):
    rows, width = scores.shape
    rb = 64
    return pl.pallas_call(
        _sample_select_kernel,
        grid=(rows // rb,),
        in_specs=[pl.BlockSpec((rb, width), lambda i: (i, 0)),
                  pl.BlockSpec((PAGE, PAGE_ROWS), lambda i: (0, 0))],
        out_specs=pl.BlockSpec((rb, out_width), lambda i: (i, 0)),
        out_shape=jax.ShapeDtypeStruct((rows, out_width), F32),
        compiler_params=_cparams("parallel"),
        name="sample_select",
    )(scores, expand)


def _sample_attn_kernel(pt_ref, q_ref, *refs, n_steps, past_len):
    pps = PAGES_PER_STEP
    k_refs, v_refs = refs[:pps], refs[pps:2 * pps]
    kn_ref, vn_ref, selp_ref, seln_ref, slope_ref, hm_ref, o_ref, m_scr, l_scr, acc_scr = refs[2 * pps:]
    jj = pl.program_id(1)

    @pl.when(jj == 0)
    def _():
        m_scr[...] = jnp.full(m_scr.shape, -3e38, F32)
        l_scr[...] = jnp.zeros(l_scr.shape, F32)
        acc_scr[...] = jnp.zeros(acc_scr.shape, F32)

    def step(k_list, v_list, sel8, pos0):
        q = q_ref[...]
        s = jnp.concatenate(
            [lax.dot_general(q, k.astype(BF16), (((1,), (1,)), ((), ())), preferred_element_type=F32)
             for k in k_list], axis=1)
        width = s.shape[1]
        lane = lax.broadcasted_iota(I32, (8, width), 1)
        rq = lax.broadcasted_iota(I32, (8, width), 0) % 4
        dist = (pos0 - past_len + lane // N_HEADS - rq).astype(F32)
        hm = jnp.concatenate([hm_ref[...]] * (width // PAGE_ROWS), axis=1) if width > PAGE_ROWS \
            else hm_ref[:, :width]
        g = N_SROWS // 8
        x = s.reshape(g, 8, width) + slope_ref[...].reshape(g, 8, 1) * dist[None]
        x = jnp.where(sel8[None] > 0.5, x, NEG_BIG) + hm.reshape(g, 8, width)
        x = x.reshape(N_SROWS, width)
        m = m_scr[...]
        m_new = jnp.maximum(m, jnp.max(x, axis=1, keepdims=True))
        alpha = jnp.exp2(m - m_new)
        p = jnp.exp2(x - m_new)
        l_scr[...] = alpha * l_scr[...] + jnp.sum(p, axis=1, keepdims=True)
        pb = p.astype(BF16)
        pv = None
        off = 0
        for v in v_list:
            part = jnp.dot(pb[:, off:off + v.shape[0]], v.astype(BF16), preferred_element_type=F32)
            pv = part if pv is None else pv + part
            off += v.shape[0]
        acc_scr[...] = alpha * acc_scr[...] + pv
        m_scr[...] = m_new

    ppg = PAGES_PER_GROUP
    for g in range(pps // ppg):
        lanes = slice(g * ppg * PAGE_ROWS, (g + 1) * ppg * PAGE_ROWS)
        step([r[...] for r in k_refs[g * ppg:(g + 1) * ppg]], [r[...] for r in v_refs[g * ppg:(g + 1) * ppg]],
             selp_ref[:, lanes], (jj * pps + g * ppg) * PAGE)

    @pl.when(jj == n_steps - 1)
    def _():
        step([kn_ref[...]], [vn_ref[...]], seln_ref[...], past_len)
        o_ref[...] = acc_scr[...] / l_scr[...]


def _sample_attention(page_table, q_s, cache_k2, cache_v2, knew, vnew, sel_exp, slope_rows, head_mask):
    b, n_pages = page_table.shape
    pps = PAGES_PER_STEP
    assert n_pages % pps == 0 and pps % PAGES_PER_GROUP == 0
    n_steps = n_pages // pps
    nw = knew.shape[1]

    def page_index(s, jj, pt, *, u):
        return (pt[s, jj * pps + u], 0)

    page_specs = [pl.BlockSpec((PAGE_ROWS, HEAD_DIM), functools.partial(page_index, u=u)) for u in range(pps)]
    grid_spec = pltpu.PrefetchScalarGridSpec(
        num_scalar_prefetch=1,
        grid=(b, n_steps),
        in_specs=[pl.BlockSpec((None, N_SROWS, HEAD_DIM), lambda s, jj, pt: (s, 0, 0))]
        + page_specs + page_specs
        + [pl.BlockSpec((None, nw, HEAD_DIM), lambda s, jj, pt: (s, 0, 0)),
           pl.BlockSpec((None, nw, HEAD_DIM), lambda s, jj, pt: (s, 0, 0)),
           pl.BlockSpec((None, 8, pps * PAGE_ROWS), lambda s, jj, pt: (s, 0, jj)),
           pl.BlockSpec((None, 8, nw), lambda s, jj, pt: (s, 0, n_pages * PAGE_ROWS // nw)),
           pl.BlockSpec((N_SROWS, 1), lambda s, jj, pt: (0, 0)),
           pl.BlockSpec((N_SROWS, PAGE_ROWS), lambda s, jj, pt: (0, 0))],
        out_specs=pl.BlockSpec((None, N_SROWS, HEAD_DIM), lambda s, jj, pt: (s, 0, 0)),
        scratch_shapes=[pltpu.VMEM((N_SROWS, 1), F32), pltpu.VMEM((N_SROWS, 1), F32),
                        pltpu.VMEM((N_SROWS, HEAD_DIM), F32)],
    )
    return pl.pallas_call(
        functools.partial(_sample_attn_kernel, n_steps=n_steps, past_len=n_pages * PAGE),
        grid_spec=grid_spec,
        out_shape=jax.ShapeDtypeStruct((b, N_SROWS, HEAD_DIM), F32),
        compiler_params=_cparams("arbitrary", "arbitrary"),
        name="sample_attention",
    )(page_table, q_s, *([cache_k2] * pps), *([cache_v2] * pps), knew, vnew, sel_exp, sel_exp, slope_rows,
      head_mask)


def _merge_kernel(ot_ref, os_ref, ga_ref, w_ref, pa_ref, g1_ref, g2_ref, out_ref, *, n_prompt_blocks):
    o = jnp.where(pl.program_id(0) < n_prompt_blocks, ot_ref[...].T, os_ref[...])
    ga = ga_ref[...]
    og = (o * (ga * _sigmoid(ga))).astype(BF16)
    pb = jnp.dot(og, w_ref[...], preferred_element_type=F32)
    merged = _sigmoid(g1_ref[...]) * pa_ref[...] + _sigmoid(g2_ref[...]) * pb
    out_ref[...] = merged.astype(out_ref.dtype)


def _merge(ot, o_s, ga, w_pb, pa, gg):
    m, d = ga.shape
    npb = ot.shape[1] // ROW_BLOCK
    row = lambda i: (i, 0)
    return pl.pallas_call(
        functools.partial(_merge_kernel, n_prompt_blocks=npb),
        grid=(m // ROW_BLOCK,),
        in_specs=[pl.BlockSpec((d, ROW_BLOCK), lambda i: (0, jnp.minimum(i, npb - 1))),
                  pl.BlockSpec((ROW_BLOCK, d), lambda i: (jnp.maximum(i - npb, 0), 0)),
                  pl.BlockSpec((ROW_BLOCK, d), row),
                  pl.BlockSpec((d, d), lambda i: (0, 0)),
                  pl.BlockSpec((ROW_BLOCK, d), row), pl.BlockSpec((ROW_BLOCK, d), row),
                  pl.BlockSpec((ROW_BLOCK, d), lambda i: (i, 1))],
        out_specs=pl.BlockSpec((ROW_BLOCK, d), row),
        out_shape=jax.ShapeDtypeStruct((m, d), BF16),
        compiler_params=_cparams("parallel"),
        name="merge",
    )(ot, o_s, ga, w_pb, pa, gg, gg)


def _out_kernel(mg_ref, w_ref, x_ref, g_ref, y_ref):
    res = x_ref[...] + jnp.dot(mg_ref[...], w_ref[...], preferred_element_type=F32)
    y = res * lax.rsqrt(jnp.mean(res * res, axis=-1, keepdims=True) + NORM_EPS)
    y_ref[...] = y * g_ref[...]


def _out_proj(merged, w_out, x, g):
    m, d = x.shape
    row = lambda i: (i, 0)
    return pl.pallas_call(
        _out_kernel,
        grid=(m // ROW_BLOCK,),
        in_specs=[pl.BlockSpec((ROW_BLOCK, d), row), pl.BlockSpec((d, d), lambda i: (0, 0)),
                  pl.BlockSpec((ROW_BLOCK, d), row), pl.BlockSpec((1, d), lambda i: (0, 0))],
        out_specs=pl.BlockSpec((ROW_BLOCK, d), row),
        out_shape=jax.ShapeDtypeStruct((m, d), F32),
        compiler_params=_cparams("parallel"),
        name="out_proj",
    )(merged, w_out, x, g.reshape(1, d))


def kernel(x_prompt, x_sample, cache_k, cache_v, cache_kidx, state_h, state_conv, page_table, meta_tokens,
           norm_g, w_in, conv_w, conv_b, lru_wa, lru_ba, lru_wx, lru_bx, lru_lambda, w_proj_a, w_proj_b,
           w_out, final_g):
    assert x_prompt.shape[0] == 1 and norm_g.shape[0] == 1
    d = D_MODEL
    seq = x_prompt.shape[1]
    t_p = seq + N_META
    t_pad = -(-t_p // Q_BLOCK) * Q_BLOCK
    n_seq, n_new = x_sample.shape[:2]
    n_s = n_seq * n_new
    n_pages = page_table.shape[1]
    past_len = n_pages * PAGE
    assert n_new == 4 and n_s % ROW_BLOCK == 0
    m_all = t_pad + n_s

    x_all = jnp.concatenate([meta_tokens.astype(F32), x_prompt[0], jnp.zeros((t_pad - t_p, d), F32),
                             x_sample.reshape(n_s, d)], axis=0)
    xn = _rmsnorm_bf16(x_all, norm_g[0])

    w = w_in[0]
    cuts = np.cumsum([d, d, d, d, d, d, N_IDX_HEADS * IDX_DIM, IDX_DIM, N_IDX_HEADS, d, d]).tolist()
    n_kw = cuts[8] - cuts[6]
    wb = jnp.concatenate([w[:, :cuts[6]], w[:, cuts[8]:], w[:, cuts[6]:cuts[8]],
                          jnp.zeros((d, 128 - n_kw), F32)], axis=1).astype(BF16)
    c_gg = cuts[6]
    c_kw = c_gg + 2 * d
    rg = _matmul(xn, wb, F32, cols=(0, cuts[1]), name="proj_rnn")
    qv = _matmul(xn, wb, BF16, cols=(cuts[1], cuts[2]), scale=LOG2E * HEAD_DIM ** -0.5, name="proj_q")
    kv, kv_bf = _matmul(xn, wb, (F32, BF16), cols=(cuts[2], cuts[4]), name="proj_kv")
    ga = _matmul(xn, wb, F32, cols=(cuts[4], cuts[5]), name="proj_ga")
    qi = _matmul(xn, wb, BF16, cols=(cuts[5], cuts[6]), scale=IDX_DIM ** -0.5, name="proj_qi")
    kw = _matmul(xn, wb, F32, cols=(c_kw, c_kw + 128), name="proj_kiwi")
    gg = _matmul(xn, wb, F32, cols=(c_gg, c_kw), name="proj_gates")

    xr, gr = rg[:, :d], rg[:, d:]
    k_all, v_all = kv[:, :d], kv[:, d:]
    ki_all = kw[:, :IDX_DIM]
    wi_all = kw[:, IDX_DIM:IDX_DIM + N_IDX_HEADS] * (N_IDX_HEADS ** -0.5)

    cw, cb = conv_w[0], conv_b[0].reshape(1, d)
    wa, wx = lru_wa[0].astype(BF16), lru_wx[0].astype(BF16)
    ba, bx, lam = lru_ba[0].reshape(1, d), lru_bx[0].reshape(1, d), lru_lambda[0].reshape(1, d)
    hs_p, y_p = _lru_prompt(rg, cw, cb, wa, wx, ba, bx, lam, t_pad)
    xr_s = jnp.swapaxes(xr[t_pad:].reshape(n_seq, n_new, d), 0, 1)
    gr_s = jnp.swapaxes(gr[t_pad:].reshape(n_seq, n_new, d), 0, 1)
    xs = jnp.concatenate([jnp.swapaxes(state_conv[0], 0, 1), xr_s], axis=0)
    hs_s, y_s = _lru_sample(xs, gr_s, state_h[0], cw, cb, wa, wx, ba, bx, lam)
    y_rnn = jnp.concatenate([y_p, jnp.swapaxes(y_s, 0, 1).reshape(n_s, d)], axis=0)

    slopes = jnp.asarray(LOG2E * 2.0 ** (-8.0 * np.arange(1, N_HEADS + 1) / N_HEADS), dtype=F32)
    qit = qi[:t_pad].T
    wt = wi_all[:t_pad].T
    qt = qv[:t_pad].T
    vt4 = kv_bf[:t_pad, d:].reshape(t_pad // S_CHUNK, S_CHUNK, N_HEADS, HEAD_DIM).transpose(2, 0, 3, 1)
    slope_b = jnp.broadcast_to(slopes[:, None, None], (N_HEADS, 8, Q_BLOCK))
    ot = _prompt_attention(qit, wt, ki_all[:t_pad].astype(BF16), qt, kv_bf, vt4, slope_b, t_pad)

    hq = lambda a, width: a.reshape(n_seq, n_new, N_HEADS, width).transpose(0, 2, 1, 3).reshape(
        n_seq, N_HEADS * n_new, width)
    qi_s = hq(qi[t_pad:], IDX_DIM)
    w_s = jnp.broadcast_to(
        wi_all[t_pad:].reshape(n_seq, n_new, N_IDX_HEADS).transpose(0, 2, 1).reshape(n_seq, -1, 1),
        (n_seq, N_IDX_HEADS * n_new, PAGE))
    kinew_t = jnp.pad(jnp.swapaxes(ki_all[t_pad:].reshape(n_seq, n_new, IDX_DIM), 1, 2),
                      ((0, 0), (0, 0), (0, PAGE - n_new)))
    scores = _sample_scores(page_table, jnp.swapaxes(cache_kidx, 2, 3), kinew_t, qi_s, w_s)
    col = np.arange(PAGE_ROWS)
    expand = jnp.asarray(col[None, :] // N_HEADS == np.arange(PAGE)[:, None], dtype=BF16)
    head_mask = jnp.asarray(np.where(np.arange(N_SROWS)[:, None] // n_new == col[None, :] % N_HEADS,
                                     0.0, 4.0 * NEG_BIG), dtype=F32)
    sel_exp = _sample_select(scores.reshape(n_seq * 8, past_len + PAGE), expand, (past_len + 8) * N_HEADS)
    sel_exp = sel_exp.reshape(n_seq, 8, (past_len + 8) * N_HEADS)
    new_rows = lambda a: jnp.pad(a.reshape(n_seq, n_new * N_HEADS, HEAD_DIM),
                                 ((0, 0), (0, (8 - n_new) * N_HEADS), (0, 0)))
    slope_rows = jnp.repeat(slopes, n_new).reshape(N_SROWS, 1)
    o_s = _sample_attention(page_table, hq(qv[t_pad:], HEAD_DIM),
                            cache_k.reshape(-1, HEAD_DIM), cache_v.reshape(-1, HEAD_DIM),
                            new_rows(k_all[t_pad:]), new_rows(v_all[t_pad:]), sel_exp, slope_rows, head_mask)
    o_s = o_s.reshape(n_seq, N_HEADS, n_new, HEAD_DIM).transpose(0, 2, 1, 3).reshape(n_s, d)

    pa = _matmul(y_rnn, w_proj_a[0].astype(BF16), F32, name="proj_a")
    merged = _merge(ot, o_s, ga, w_proj_b[0].astype(BF16), pa, gg)
    y_all = _out_proj(merged, w_out[0].astype(BF16), x_all, final_g)

    y_prompt = y_all[N_META:t_p][None]
    y_sample = y_all[t_pad:].reshape(n_seq, n_new, d)
    heads = lambda a, *lead: a.reshape(*lead, N_HEADS, HEAD_DIM)
    k_prompt = heads(k_all[:t_p], 1, 1, t_p)
    v_prompt = heads(v_all[:t_p], 1, 1, t_p)
    kidx_prompt = ki_all[:t_p][None, None]
    h_prompt = hs_p[t_p - 1][None, None]
    conv_prompt = xr[t_p - (CONV_W - 1):t_p][None, None]
    k_sample = heads(k_all[t_pad:], 1, n_seq, n_new)
    v_sample = heads(v_all[t_pad:], 1, n_seq, n_new)
    kidx_sample = ki_all[t_pad:].reshape(1, n_seq, n_new, IDX_DIM)
    h_sample = hs_s[n_new - 1][None]
    conv_sample = jnp.swapaxes(xs[n_new:], 0, 1)[None]
    return (y_prompt, y_sample, k_prompt, v_prompt, kidx_prompt, h_prompt, conv_prompt,
            k_sample, v_sample, kidx_sample, h_sample, conv_sample)
```

```python
import functools

import numpy as np
import jax
import jax.numpy as jnp
from jax import lax
from jax.experimental import pallas as pl
from jax.experimental.pallas import tpu as pltpu

F32 = jnp.float32
BF16 = jnp.bfloat16
I32 = jnp.int32

D_MODEL = 2048
N_HEADS = 16
HEAD_DIM = 128
N_IDX_HEADS = 16
IDX_DIM = 64
N_LRU_BLOCKS = 16
LRU_BLOCK = 128
CONV_W = 4
LRU_C = 8.0
N_META = 16
TOPK = 256
PAGE = 128
NORM_EPS = 1e-6
NEG_BIG = -1e30
INT_MIN = -(2 ** 31)

ROW_BLOCK = 256
Q_BLOCK = 256
S_CHUNK = 256
KEY_CHUNKS = (2048, 1024, S_CHUNK)
HEADS_PER_STEP = 2
LOG2E = 1.4426950408889634
VMEM_LIMIT = 56 * 1024 * 1024


def _cparams(*sem):
    return pltpu.CompilerParams(dimension_semantics=sem, vmem_limit_bytes=VMEM_LIMIT)


def _rmsnorm_kernel(x_ref, g_ref, o_ref):
    x = x_ref[...]
    y = x * lax.rsqrt(jnp.mean(x * x, axis=-1, keepdims=True) + NORM_EPS)
    o_ref[...] = (y * g_ref[...]).astype(o_ref.dtype)


def _rmsnorm_bf16(x, g):
    m, d = x.shape
    return pl.pallas_call(
        _rmsnorm_kernel,
        grid=(m // ROW_BLOCK,),
        in_specs=[pl.BlockSpec((ROW_BLOCK, d), lambda i: (i, 0)),
                  pl.BlockSpec((1, d), lambda i: (0, 0))],
        out_specs=pl.BlockSpec((ROW_BLOCK, d), lambda i: (i, 0)),
        out_shape=jax.ShapeDtypeStruct((m, d), BF16),
        compiler_params=_cparams("parallel"),
        name="rmsnorm_in",
    )(x, g.reshape(1, d))


def _mm_kernel(x_ref, w_ref, *o_refs, scale):
    acc = jnp.dot(x_ref[...], w_ref[...], preferred_element_type=F32)
    if scale is not None:
        acc = acc * scale
    for o_ref in o_refs:
        o_ref[...] = acc.astype(o_ref.dtype)


def _matmul(x, w, out_dtypes, *, cols=None, scale=None, tm=1280, tn=1024, name="matmul"):
    m, k = x.shape
    lo, hi = (0, w.shape[1]) if cols is None else cols
    n = hi - lo
    tn = min(tn, n)
    assert m % tm == 0 and n % tn == 0 and lo % tn == 0
    j0 = lo // tn
    single = not isinstance(out_dtypes, tuple)
    dts = (out_dtypes,) if single else out_dtypes
    outs = pl.pallas_call(
        functools.partial(_mm_kernel, scale=scale),
        grid=(n // tn, m // tm),
        in_specs=[pl.BlockSpec((tm, k), lambda j, i: (i, 0)),
                  pl.BlockSpec((k, tn), lambda j, i: (0, j0 + j))],
        out_specs=[pl.BlockSpec((tm, tn), lambda j, i: (i, j)) for _ in dts],
        out_shape=[jax.ShapeDtypeStruct((m, n), dt) for dt in dts],
        compiler_params=_cparams("parallel", "parallel"),
        name=name,
    )(x, w)
    return outs[0] if single else outs


def _softplus(x):
    return jnp.maximum(x, 0.0) + jnp.log1p(jnp.exp(-jnp.abs(x)))


def _sigmoid(x):
    return 0.5 * jnp.tanh(0.5 * x) + 0.5


def _lru_coeffs(xc, wa_ref, wx_ref, ba, bx, lam):
    rs, is_ = [], []
    for n in range(N_LRU_BLOCKS):
        sl = slice(n * LRU_BLOCK, (n + 1) * LRU_BLOCK)
        xb = xc[:, sl].astype(BF16)
        rs.append(jnp.dot(xb, wa_ref[n], preferred_element_type=F32))
        is_.append(jnp.dot(xb, wx_ref[n], preferred_element_type=F32))
    r = _sigmoid(jnp.concatenate(rs, axis=-1) + ba)
    i = _sigmoid(jnp.concatenate(is_, axis=-1) + bx)
    log_a = (-LRU_C * _softplus(-lam)) * r
    a = jnp.exp(log_a)
    u = jnp.sqrt(1.0 - a * a) * (i * xc)
    return a, u


def _lru_prompt_kernel(xr_ref, gr_ref, cw_ref, cb_ref, wa_ref, wx_ref, ba_ref, bx_ref, lam_ref,
                       hs_ref, y_ref, xp_scr, a_scr, u_scr, h_scr):
    tb = ROW_BLOCK

    @pl.when(pl.program_id(0) == 0)
    def _():
        xp_scr[0:8, :] = jnp.zeros((8, D_MODEL), F32)
        h_scr[...] = jnp.zeros((8, D_MODEL), F32)

    x = xr_ref[...]
    xp_scr[8:8 + tb, :] = x
    xc = cb_ref[...] + xp_scr[5:5 + tb, :] * cw_ref[0:1, :]
    xc = xc + xp_scr[6:6 + tb, :] * cw_ref[1:2, :]
    xc = xc + xp_scr[7:7 + tb, :] * cw_ref[2:3, :]
    xc = xc + x * cw_ref[3:4, :]
    xp_scr[0:8, :] = x[tb - 8:tb, :]

    a, u = _lru_coeffs(xc, wa_ref, wx_ref, ba_ref[...], bx_ref[...], lam_ref[...])
    a_scr[...] = a
    u_scr[...] = u

    def group(g, h):
        r0 = pl.multiple_of(g * 8, 8)
        a8 = a_scr[pl.ds(r0, 8), :]
        u8 = u_scr[pl.ds(r0, 8), :]
        rows = []
        for r in range(8):
            h = a8[r:r + 1, :] * h + u8[r:r + 1, :]
            rows.append(h)
        hs_ref[pl.ds(r0, 8), :] = jnp.concatenate(rows, axis=0)
        return h

    h_last = lax.fori_loop(0, tb // 8, group, h_scr[0:1, :])
    h_scr[0:1, :] = h_last

    g = gr_ref[...]
    y_ref[...] = (hs_ref[...] * (g * _sigmoid(g))).astype(y_ref.dtype)


def _lru_prompt(rg, cw, cb, wa, wx, ba, bx, lam, n_rows):
    d = D_MODEL
    row = lambda i: (i, 0)
    const2 = lambda i: (0, 0)
    const3 = lambda i: (0, 0, 0)
    return pl.pallas_call(
        _lru_prompt_kernel,
        grid=(n_rows // ROW_BLOCK,),
        in_specs=[pl.BlockSpec((ROW_BLOCK, d), row), pl.BlockSpec((ROW_BLOCK, d), lambda i: (i, 1)),
                  pl.BlockSpec((CONV_W, d), const2), pl.BlockSpec((1, d), const2),
                  pl.BlockSpec((N_LRU_BLOCKS, LRU_BLOCK, LRU_BLOCK), const3),
                  pl.BlockSpec((N_LRU_BLOCKS, LRU_BLOCK, LRU_BLOCK), const3),
                  pl.BlockSpec((1, d), const2), pl.BlockSpec((1, d), const2), pl.BlockSpec((1, d), const2)],
        out_specs=[pl.BlockSpec((ROW_BLOCK, d), row), pl.BlockSpec((ROW_BLOCK, d), row)],
        out_shape=[jax.ShapeDtypeStruct((n_rows, d), F32), jax.ShapeDtypeStruct((n_rows, d), BF16)],
        scratch_shapes=[pltpu.VMEM((8 + ROW_BLOCK, d), F32), pltpu.VMEM((ROW_BLOCK, d), F32),
                        pltpu.VMEM((ROW_BLOCK, d), F32), pltpu.VMEM((8, d), F32)],
        compiler_params=_cparams("arbitrary"),
        name="lru_prompt",
    )(rg, rg, cw, cb, wa, wx, ba, bx, lam)


def _lru_sample_kernel(xs_ref, gr_ref, h0_ref, cw_ref, cb_ref, wa_ref, wx_ref, ba_ref, bx_ref, lam_ref,
                       hs_ref, y_ref):
    n_t = hs_ref.shape[0]
    h = h0_ref[...]
    for t in range(n_t):
        xc = cb_ref[...] + xs_ref[t] * cw_ref[0:1, :]
        for j in range(1, CONV_W):
            xc = xc + xs_ref[t + j] * cw_ref[j:j + 1, :]
        a, u = _lru_coeffs(xc, wa_ref, wx_ref, ba_ref[...], bx_ref[...], lam_ref[...])
        h = a * h + u
        hs_ref[t] = h
        g = gr_ref[t]
        y_ref[t] = (h * (g * _sigmoid(g))).astype(y_ref.dtype)


def _lru_sample(xs, gr, h0, cw, cb, wa, wx, ba, bx, lam):
    n_t, b, d = gr.shape
    return pl.pallas_call(
        _lru_sample_kernel,
        out_shape=[jax.ShapeDtypeStruct((n_t, b, d), F32), jax.ShapeDtypeStruct((n_t, b, d), BF16)],
        compiler_params=pltpu.CompilerParams(vmem_limit_bytes=VMEM_LIMIT),
        name="lru_sample",
    )(xs, gr, h0, cw, cb, wa, wx, ba, bx, lam)


LOWEST_KEY = INT_MIN + 0x00800000
TIE_ROW_BITS = 14
TIE_ROW_LIMIT = 2 ** TIE_ROW_BITS - 1


def _key_to_f32(key):
    return pltpu.bitcast(key ^ ((key >> 31) & 0x7FFFFFFF), F32)


def _bisect_threshold(count_ge, zero):
    c0 = count_ge(jnp.zeros(zero.shape, F32))
    thr = jnp.where(c0 >= TOPK, zero, zero + INT_MIN)
    cnt = jnp.where(c0 >= TOPK, c0, 0.0)

    def body(p, carry):
        thr, cnt = carry
        cand = thr + jnp.left_shift(jnp.int32(1), 30 - p)
        c = count_ge(_key_to_f32(cand))
        return jnp.where(c >= TOPK, cand, thr), jnp.where(c >= TOPK, c, cnt)

    thr, cnt = lax.fori_loop(0, 31, body, (thr, cnt))
    return _key_to_f32(jnp.maximum(thr, LOWEST_KEY)), cnt


def _for_key_rows(n_small, body, carry):
    done = 0
    for size in KEY_CHUNKS:
        n_this = (n_small * S_CHUNK - done) // size
        carry = lax.fori_loop(
            0, n_this, lambda c, x, d=done, sz=size: body(pl.multiple_of(d + c * sz, sz), sz, x), carry)
        done = done + n_this * size
    return carry


def _prompt_attn_kernel(qit_ref, wt_ref, ki_ref, qt_ref, k_ref, vt_ref, slope_ref, o_ref, sel_scr, s_scr,
                        tie_scr):
    i = pl.program_id(0)
    h = pl.program_id(1)
    hps = HEADS_PER_STEP
    n_chunks = i + 1
    t0 = i * Q_BLOCK
    sc, qb = S_CHUNK, Q_BLOCK

    @pl.when(h == 0)
    def _select():
        def score_chunk(c, carry):
            r0 = pl.multiple_of(c * sc, sc)
            kc = ki_ref[pl.ds(r0, sc), :]
            acc = jnp.zeros((sc, qb), F32)
            for hh in range(N_IDX_HEADS):
                s = jnp.dot(kc, qit_ref[hh * IDX_DIM:(hh + 1) * IDX_DIM, :], preferred_element_type=F32)
                acc = acc + jnp.maximum(s, 0.0) * wt_ref[hh:hh + 1, :]
            row = r0 + lax.broadcasted_iota(I32, (sc, qb), 0)
            col = t0 + lax.broadcasted_iota(I32, (sc, qb), 1)
            sel_scr[pl.ds(r0, sc), :] = jnp.where(row <= col, acc, -jnp.inf)
            return carry

        lax.fori_loop(0, n_chunks, score_chunk, 0)

        def count_ge(cand):
            def body(r0, rows, cnt):
                ind = jnp.where(sel_scr[pl.ds(r0, rows), :] >= cand, 1.0, 0.0)
                return cnt + ind.reshape(rows // 8, 8, qb).sum(axis=0)

            cnt8 = _for_key_rows(n_chunks, body, jnp.zeros((8, qb), F32))
            return cnt8.sum(axis=0, keepdims=True)

        thr, n_ge = _bisect_threshold(count_ge, jnp.zeros((1, qb), I32))

        def count_rows(indicator):
            def body(r0, rows, cnt):
                row = r0 + lax.broadcasted_iota(I32, (rows, qb), 0)
                ind = indicator(sel_scr[pl.ds(r0, rows), :], row)
                return cnt + ind.reshape(rows // 8, 8, qb).sum(axis=0)

            return _for_key_rows(n_chunks, body, jnp.zeros((8, qb), F32)).sum(axis=0, keepdims=True)

        tie_scr[...] = jnp.full(tie_scr.shape, TIE_ROW_LIMIT, I32)

        @pl.when(jnp.max(n_ge) > TOPK)
        def _():
            need = TOPK - count_rows(lambda x, row: jnp.where(x > thr, 1.0, 0.0))

            def body(p, last):
                cand = last + jnp.left_shift(jnp.int32(1), TIE_ROW_BITS - 1 - p)
                below = count_rows(lambda x, row: jnp.where(x == thr, jnp.where(row < cand, 1.0, 0.0), 0.0))
                return jnp.where(below < need, cand, last)

            tie_scr[0:1, :] = lax.fori_loop(0, TIE_ROW_BITS, body, jnp.zeros((1, qb), I32))

        last = tie_scr[0:1, :]

        def mark(r0, rows, carry):
            row = r0 + lax.broadcasted_iota(I32, (rows, qb), 0)
            col = t0 + lax.broadcasted_iota(I32, (rows, qb), 1)
            x = sel_scr[pl.ds(r0, rows), :]
            dist = (row - col).astype(F32)
            sel_scr[pl.ds(r0, rows), :] = jnp.where(
                x == thr, jnp.where(row <= last, dist, NEG_BIG), jnp.where(x > thr, dist, NEG_BIG))
            return carry

        _for_key_rows(n_chunks, mark, 0)

    qh = [qt_ref[u * HEAD_DIM:(u + 1) * HEAD_DIM, :] for u in range(hps)]
    slope = [slope_ref[u, 0:1, :] for u in range(hps)]

    def logits(r0, rows, mx):
        kk = k_ref[pl.ds(r0, rows), :]
        nd = sel_scr[pl.ds(r0, rows), :]
        out = []
        for u in range(hps):
            s = jnp.dot(kk[:, u * HEAD_DIM:(u + 1) * HEAD_DIM], qh[u], preferred_element_type=F32)
            s = s + nd * slope[u]
            s_scr[u, pl.ds(r0, rows), :] = s
            out.append(jnp.maximum(mx[u], s.reshape(rows // 8, 8, qb).max(axis=0)))
        return tuple(out)

    mx8 = _for_key_rows(n_chunks, logits, tuple(jnp.full((8, qb), -3e38, F32) for _ in range(hps)))
    m = [x.max(axis=0, keepdims=True) for x in mx8]

    def weigh(r0, rows, carry):
        c0 = r0 // sc
        out = []
        for u in range(hps):
            l8, acc = carry[u]
            p = jnp.exp2(s_scr[u, pl.ds(r0, rows), :] - m[u])
            l8 = l8 + p.reshape(rows // 8, 8, qb).sum(axis=0)
            v = jnp.concatenate([vt_ref[u, c0 + j] for j in range(rows // sc)], axis=1)
            acc = acc + jnp.dot(v, p.astype(BF16), preferred_element_type=F32)
            out.append((l8, acc))
        return tuple(out)

    zero = (jnp.zeros((8, qb), F32), jnp.zeros((HEAD_DIM, qb), F32))
    res = _for_key_rows(n_chunks, weigh, tuple(zero for _ in range(hps)))
    for u in range(hps):
        l8, acc = res[u]
        o_ref[u * HEAD_DIM:(u + 1) * HEAD_DIM, :] = acc / l8.sum(axis=0, keepdims=True)


def _prompt_attention(qit, wt, ki, qt, kv_bf, vt4, slopes, t_pad):
    n_blk = t_pad // Q_BLOCK
    hps = HEADS_PER_STEP
    assert t_pad <= TIE_ROW_LIMIT and t_pad % KEY_CHUNKS[-1] == 0
    return pl.pallas_call(
        _prompt_attn_kernel,
        grid=(n_blk, N_HEADS // hps),
        in_specs=[pl.BlockSpec((N_IDX_HEADS * IDX_DIM, Q_BLOCK), lambda i, h: (0, i)),
                  pl.BlockSpec((N_IDX_HEADS, Q_BLOCK), lambda i, h: (0, i)),
                  pl.BlockSpec((t_pad, IDX_DIM), lambda i, h: (0, 0)),
                  pl.BlockSpec((hps * HEAD_DIM, Q_BLOCK), lambda i, h: (h, i)),
                  pl.BlockSpec((t_pad, hps * HEAD_DIM), lambda i, h: (0, h)),
                  pl.BlockSpec((hps, t_pad // S_CHUNK, HEAD_DIM, S_CHUNK), lambda i, h: (h, 0, 0, 0)),
                  pl.BlockSpec((hps, 8, Q_BLOCK), lambda i, h: (h, 0, 0))],
        out_specs=pl.BlockSpec((hps * HEAD_DIM, Q_BLOCK), lambda i, h: (h, i)),
        out_shape=jax.ShapeDtypeStruct((N_HEADS * HEAD_DIM, t_pad), F32),
        scratch_shapes=[pltpu.VMEM((t_pad, Q_BLOCK), F32), pltpu.VMEM((hps, t_pad, Q_BLOCK), F32),
                        pltpu.VMEM((8, Q_BLOCK), I32)],
        compiler_params=_cparams("arbitrary", "arbitrary"),
        name="prompt_attention",
    )(qit, wt, ki, qt, kv_bf, vt4, slopes)


N_SROWS = 4 * N_HEADS
PAGE_ROWS = PAGE * N_HEADS
PAGES_PER_STEP = 8
PAGES_PER_GROUP = 4


def _sample_score_kernel(pt_ref, *refs, n_pages):
    kidx_refs = refs[:n_pages]
    kinew_ref, qi_ref, w_ref, o_ref = refs[n_pages:]
    qi = qi_ref[...]
    w = w_ref[...]

    def page_scores(page):
        s = jnp.dot(qi, page.astype(BF16), preferred_element_type=F32)
        x = jnp.maximum(s, 0.0) * w
        x8 = x.reshape(N_IDX_HEADS // 2, 8, PAGE).sum(axis=0)
        return x8 + pltpu.roll(x8, 4, 0)

    for p in range(n_pages):
        o_ref[:, p * PAGE:(p + 1) * PAGE] = page_scores(kidx_refs[p][...])
    slot = lax.broadcasted_iota(I32, (8, PAGE), 1)
    q = lax.broadcasted_iota(I32, (8, PAGE), 0) % 4
    o_ref[:, n_pages * PAGE:] = jnp.where(slot <= q, page_scores(kinew_ref[...]), NEG_BIG)


def _sample_scores(page_table, cache_kidx_t, kinew_t, qi_s, w_s):
    b, n_pages = page_table.shape

    def page_index(s, pt, *, p):
        return (0, pt[s, p], 0, 0)

    grid_spec = pltpu.PrefetchScalarGridSpec(
        num_scalar_prefetch=1,
        grid=(b,),
        in_specs=[pl.BlockSpec((None, None, IDX_DIM, PAGE), functools.partial(page_index, p=p))
                  for p in range(n_pages)]
        + [pl.BlockSpec((None, IDX_DIM, PAGE), lambda s, pt: (s, 0, 0)),
           pl.BlockSpec((None, N_SROWS, IDX_DIM), lambda s, pt: (s, 0, 0)),
           pl.BlockSpec((None, N_SROWS, PAGE), lambda s, pt: (s, 0, 0))],
        out_specs=pl.BlockSpec((None, 8, (n_pages + 1) * PAGE), lambda s, pt: (s, 0, 0)),
    )
    return pl.pallas_call(
        functools.partial(_sample_score_kernel, n_pages=n_pages),
        grid_spec=grid_spec,
        out_shape=jax.ShapeDtypeStruct((b, 8, (n_pages + 1) * PAGE), F32),
        compiler_params=_cparams("arbitrary"),
        name="sample_scores",
    )(page_table, *([cache_kidx_t] * n_pages), kinew_t, qi_s, w_s)


def _sample_select_kernel(s_ref, e_ref, o_ref, tie_scr):
    sc = s_ref[...]
    rows, width = sc.shape

    def count_ge(cand):
        return jnp.sum(jnp.where(sc >= cand, 1.0, 0.0), axis=1, keepdims=True)

    thr, n_ge = _bisect_threshold(count_ge, jnp.zeros((rows, 1), I32))
    pos = lax.broadcasted_iota(I32, (rows, width), 1)
    tie_scr[...] = jnp.full(tie_scr.shape, TIE_ROW_LIMIT, I32)

    @pl.when(jnp.max(n_ge) > TOPK)
    def _():
        need = TOPK - jnp.sum(jnp.where(sc > thr, 1.0, 0.0), axis=1, keepdims=True)

        def tie_body(p, last):
            cand = last + jnp.left_shift(jnp.int32(1), TIE_ROW_BITS - 1 - p)
            below = jnp.sum(jnp.where(sc == thr, jnp.where(pos < cand, 1.0, 0.0), 0.0), axis=1, keepdims=True)
            return jnp.where(below < need, cand, last)

        tie_scr[...] = lax.fori_loop(0, TIE_ROW_BITS, tie_body, jnp.zeros((rows, 1), I32))

    last = tie_scr[...]
    sel = jnp.where(sc == thr, jnp.where(pos <= last, 1.0, 0.0), jnp.where(sc > thr, 1.0, 0.0)).astype(BF16)
    n_tiles = width // PAGE
    for t in range(n_tiles):
        x = jnp.dot(sel[:, t * PAGE:(t + 1) * PAGE], e_ref[...], preferred_element_type=F32)
        lo = t * PAGE_ROWS
        hi = min(lo + PAGE_ROWS, o_ref.shape[1])
        o_ref[:, lo:hi] = x[:, :hi - lo]


def _sample_select(scores, expand, out_width):
    rows, width = scores.shape
    rb = 64
    return pl.pallas_call(
        _sample_select_kernel,
        grid=(rows // rb,),
        in_specs=[pl.BlockSpec((rb, width), lambda i: (i, 0)),
                  pl.BlockSpec((PAGE, PAGE_ROWS), lambda i: (0, 0))],
        out_specs=pl.BlockSpec((rb, out_width), lambda i: (i, 0)),
        out_shape=jax.ShapeDtypeStruct((rows, out_width), F32),
        scratch_shapes=[pltpu.VMEM((rb, 1), I32)],
        compiler_params=_cparams("parallel"),
        name="sample_select",
    )(scores, expand)


def _sample_attn_kernel(pt_ref, q_ref, *refs, n_steps, past_len):
    pps = PAGES_PER_STEP
    k_refs, v_refs = refs[:pps], refs[pps:2 * pps]
    kn_ref, vn_ref, selp_ref, seln_ref, slope_ref, hm_ref, o_ref, m_scr, l_scr, acc_scr = refs[2 * pps:]
    jj = pl.program_id(1)

    @pl.when(jj == 0)
    def _():
        m_scr[...] = jnp.full(m_scr.shape, -3e38, F32)
        l_scr[...] = jnp.zeros(l_scr.shape, F32)
        acc_scr[...] = jnp.zeros(acc_scr.shape, F32)

    def step(k_list, v_list, sel8, pos0):
        q = q_ref[...]
        s = jnp.concatenate(
            [lax.dot_general(q, k.astype(BF16), (((1,), (1,)), ((), ())), preferred_element_type=F32)
             for k in k_list], axis=1)
        width = s.shape[1]
        lane = lax.broadcasted_iota(I32, (8, width), 1)
        rq = lax.broadcasted_iota(I32, (8, width), 0) % 4
        dist = (pos0 - past_len + lane // N_HEADS - rq).astype(F32)
        hm = jnp.concatenate([hm_ref[...]] * (width // PAGE_ROWS), axis=1) if width > PAGE_ROWS \
            else hm_ref[:, :width]
        g = N_SROWS // 8
        x = s.reshape(g, 8, width) + slope_ref[...].reshape(g, 8, 1) * dist[None]
        x = jnp.where(sel8[None] > 0.5, x, NEG_BIG) + hm.reshape(g, 8, width)
        x = x.reshape(N_SROWS, width)
        m = m_scr[...]
        m_new = jnp.maximum(m, jnp.max(x, axis=1, keepdims=True))
        alpha = jnp.exp2(m - m_new)
        p = jnp.exp2(x - m_new)
        l_scr[...] = alpha * l_scr[...] + jnp.sum(p, axis=1, keepdims=True)
        pb = p.astype(BF16)
        pv = None
        off = 0
        for v in v_list:
            part = jnp.dot(pb[:, off:off + v.shape[0]], v.astype(BF16), preferred_element_type=F32)
            pv = part if pv is None else pv + part
            off += v.shape[0]
        acc_scr[...] = alpha * acc_scr[...] + pv
        m_scr[...] = m_new

    ppg = PAGES_PER_GROUP
    for g in range(pps // ppg):
        lanes = slice(g * ppg * PAGE_ROWS, (g + 1) * ppg * PAGE_ROWS)
        step([r[...] for r in k_refs[g * ppg:(g + 1) * ppg]], [r[...] for r in v_refs[g * ppg:(g + 1) * ppg]],
             selp_ref[:, lanes], (jj * pps + g * ppg) * PAGE)

    @pl.when(jj == n_steps - 1)
    def _():
        step([kn_ref[...]], [vn_ref[...]], seln_ref[...], past_len)
        o_ref[...] = acc_scr[...] / l_scr[...]


def _sample_attention(page_table, q_s, cache_k2, cache_v2, knew, vnew, sel_exp, slope_rows, head_mask):
    b, n_pages = page_table.shape
    pps = PAGES_PER_STEP
    assert n_pages % pps == 0 and pps % PAGES_PER_GROUP == 0
    n_steps = n_pages // pps
    nw = knew.shape[1]

    def page_index(s, jj, pt, *, u):
        return (pt[s, jj * pps + u], 0)

    page_specs = [pl.BlockSpec((PAGE_ROWS, HEAD_DIM), functools.partial(page_index, u=u)) for u in range(pps)]
    grid_spec = pltpu.PrefetchScalarGridSpec(
        num_scalar_prefetch=1,
        grid=(b, n_steps),
        in_specs=[pl.BlockSpec((None, N_SROWS, HEAD_DIM), lambda s, jj, pt: (s, 0, 0))]
        + page_specs + page_specs
        + [pl.BlockSpec((None, nw, HEAD_DIM), lambda s, jj, pt: (s, 0, 0)),
           pl.BlockSpec((None, nw, HEAD_DIM), lambda s, jj, pt: (s, 0, 0)),
           pl.BlockSpec((None, 8, pps * PAGE_ROWS), lambda s, jj, pt: (s, 0, jj)),
           pl.BlockSpec((None, 8, nw), lambda s, jj, pt: (s, 0, n_pages * PAGE_ROWS // nw)),
           pl.BlockSpec((N_SROWS, 1), lambda s, jj, pt: (0, 0)),
           pl.BlockSpec((N_SROWS, PAGE_ROWS), lambda s, jj, pt: (0, 0))],
        out_specs=pl.BlockSpec((None, N_SROWS, HEAD_DIM), lambda s, jj, pt: (s, 0, 0)),
        scratch_shapes=[pltpu.VMEM((N_SROWS, 1), F32), pltpu.VMEM((N_SROWS, 1), F32),
                        pltpu.VMEM((N_SROWS, HEAD_DIM), F32)],
    )
    return pl.pallas_call(
        functools.partial(_sample_attn_kernel, n_steps=n_steps, past_len=n_pages * PAGE),
        grid_spec=grid_spec,
        out_shape=jax.ShapeDtypeStruct((b, N_SROWS, HEAD_DIM), F32),
        compiler_params=_cparams("arbitrary", "arbitrary"),
        name="sample_attention",
    )(page_table, q_s, *([cache_k2] * pps), *([cache_v2] * pps), knew, vnew, sel_exp, sel_exp, slope_rows,
      head_mask)


def _merge_kernel(ot_ref, os_ref, ga_ref, w_ref, pa_ref, g1_ref, g2_ref, out_ref, *, n_prompt_blocks):
    o = jnp.where(pl.program_id(0) < n_prompt_blocks, ot_ref[...].T, os_ref[...])
    ga = ga_ref[...]
    og = (o * (ga * _sigmoid(ga))).astype(BF16)
    pb = jnp.dot(og, w_ref[...], preferred_element_type=F32)
    merged = _sigmoid(g1_ref[...]) * pa_ref[...] + _sigmoid(g2_ref[...]) * pb
    out_ref[...] = merged.astype(out_ref.dtype)


def _merge(ot, o_s, ga, w_pb, pa, gg):
    m, d = ga.shape
    npb = ot.shape[1] // ROW_BLOCK
    row = lambda i: (i, 0)
    return pl.pallas_call(
        functools.partial(_merge_kernel, n_prompt_blocks=npb),
        grid=(m // ROW_BLOCK,),
        in_specs=[pl.BlockSpec((d, ROW_BLOCK), lambda i: (0, jnp.minimum(i, npb - 1))),
                  pl.BlockSpec((ROW_BLOCK, d), lambda i: (jnp.maximum(i - npb, 0), 0)),
                  pl.BlockSpec((ROW_BLOCK, d), row),
                  pl.BlockSpec((d, d), lambda i: (0, 0)),
                  pl.BlockSpec((ROW_BLOCK, d), row), pl.BlockSpec((ROW_BLOCK, d), row),
                  pl.BlockSpec((ROW_BLOCK, d), lambda i: (i, 1))],
        out_specs=pl.BlockSpec((ROW_BLOCK, d), row),
        out_shape=jax.ShapeDtypeStruct((m, d), BF16),
        compiler_params=_cparams("parallel"),
        name="merge",
    )(ot, o_s, ga, w_pb, pa, gg, gg)


def _out_kernel(mg_ref, w_ref, x_ref, g_ref, y_ref):
    res = x_ref[...] + jnp.dot(mg_ref[...], w_ref[...], preferred_element_type=F32)
    y = res * lax.rsqrt(jnp.mean(res * res, axis=-1, keepdims=True) + NORM_EPS)
    y_ref[...] = y * g_ref[...]


def _out_proj(merged, w_out, x, g):
    m, d = x.shape
    row = lambda i: (i, 0)
    return pl.pallas_call(
        _out_kernel,
        grid=(m // ROW_BLOCK,),
        in_specs=[pl.BlockSpec((ROW_BLOCK, d), row), pl.BlockSpec((d, d), lambda i: (0, 0)),
                  pl.BlockSpec((ROW_BLOCK, d), row), pl.BlockSpec((1, d), lambda i: (0, 0))],
        out_specs=pl.BlockSpec((ROW_BLOCK, d), row),
        out_shape=jax.ShapeDtypeStruct((m, d), F32),
        compiler_params=_cparams("parallel"),
        name="out_proj",
    )(merged, w_out, x, g.reshape(1, d))


def kernel(x_prompt, x_sample, cache_k, cache_v, cache_kidx, state_h, state_conv, page_table, meta_tokens,
           norm_g, w_in, conv_w, conv_b, lru_wa, lru_ba, lru_wx, lru_bx, lru_lambda, w_proj_a, w_proj_b,
           w_out, final_g):
    assert x_prompt.shape[0] == 1 and norm_g.shape[0] == 1
    d = D_MODEL
    seq = x_prompt.shape[1]
    t_p = seq + N_META
    t_pad = -(-t_p // Q_BLOCK) * Q_BLOCK
    n_seq, n_new = x_sample.shape[:2]
    n_s = n_seq * n_new
    n_pages = page_table.shape[1]
    past_len = n_pages * PAGE
    assert n_new == 4 and n_s % ROW_BLOCK == 0
    m_all = t_pad + n_s

    x_all = jnp.concatenate([meta_tokens.astype(F32), x_prompt[0], jnp.zeros((t_pad - t_p, d), F32),
                             x_sample.reshape(n_s, d)], axis=0)
    xn = _rmsnorm_bf16(x_all, norm_g[0])

    wb = w_in[0].astype(BF16)
    cuts = np.cumsum([d, d, d, d, d, d, N_IDX_HEADS * IDX_DIM, IDX_DIM, N_IDX_HEADS, d, d]).tolist()
    rg = _matmul(xn, wb, F32, cols=(0, cuts[1]), name="proj_rnn")
    qv = _matmul(xn, wb, BF16, cols=(cuts[1], cuts[2]), scale=LOG2E * HEAD_DIM ** -0.5, name="proj_q")
    kv, kv_bf = _matmul(xn, wb, (F32, BF16), cols=(cuts[2], cuts[4]), name="proj_kv")
    ga = _matmul(xn, wb, F32, cols=(cuts[4], cuts[5]), name="proj_ga")
    qi = _matmul(xn, wb, BF16, cols=(cuts[5], cuts[6]), scale=IDX_DIM ** -0.5, name="proj_qi")
    w_kw = jnp.pad(wb[:, cuts[6]:cuts[8]], ((0, 0), (0, 128 - (cuts[8] - cuts[6]))))
    kw = _matmul(xn, w_kw, F32, name="proj_kiwi")
    gg = _matmul(xn, wb[:, cuts[8]:], F32, name="proj_gates")

    xr, gr = rg[:, :d], rg[:, d:]
    k_all, v_all = kv[:, :d], kv[:, d:]
    ki_all = kw[:, :IDX_DIM]
    wi_all = kw[:, IDX_DIM:IDX_DIM + N_IDX_HEADS] * (N_IDX_HEADS ** -0.5)

    cw, cb = conv_w[0], conv_b[0].reshape(1, d)
    wa, wx = lru_wa[0].astype(BF16), lru_wx[0].astype(BF16)
    ba, bx, lam = lru_ba[0].reshape(1, d), lru_bx[0].reshape(1, d), lru_lambda[0].reshape(1, d)
    hs_p, y_p = _lru_prompt(rg, cw, cb, wa, wx, ba, bx, lam, t_pad)
    xr_s = jnp.swapaxes(xr[t_pad:].reshape(n_seq, n_new, d), 0, 1)
    gr_s = jnp.swapaxes(gr[t_pad:].reshape(n_seq, n_new, d), 0, 1)
    xs = jnp.concatenate([jnp.swapaxes(state_conv[0], 0, 1), xr_s], axis=0)
    hs_s, y_s = _lru_sample(xs, gr_s, state_h[0], cw, cb, wa, wx, ba, bx, lam)
    y_rnn = jnp.concatenate([y_p, jnp.swapaxes(y_s, 0, 1).reshape(n_s, d)], axis=0)

    slopes = jnp.asarray(LOG2E * 2.0 ** (-8.0 * np.arange(1, N_HEADS + 1) / N_HEADS), dtype=F32)
    qit = qi.T
    wt = wi_all.T
    qt = qv.T
    vt4 = kv_bf[:t_pad, d:].reshape(t_pad // S_CHUNK, S_CHUNK, N_HEADS, HEAD_DIM).transpose(2, 0, 3, 1)
    slope_b = jnp.broadcast_to(slopes[:, None, None], (N_HEADS, 8, Q_BLOCK))
    ot = _prompt_attention(qit, wt, ki_all.astype(BF16), qt, kv_bf, vt4, slope_b, t_pad)

    hq = lambda a, width: a.reshape(n_seq, n_new, N_HEADS, width).transpose(0, 2, 1, 3).reshape(
        n_seq, N_HEADS * n_new, width)
    qi_s = hq(qi[t_pad:], IDX_DIM)
    w_s = jnp.broadcast_to(
        wi_all[t_pad:].reshape(n_seq, n_new, N_IDX_HEADS).transpose(0, 2, 1).reshape(n_seq, -1, 1),
        (n_seq, N_IDX_HEADS * n_new, PAGE))
    kinew_t = jnp.pad(jnp.swapaxes(ki_all[t_pad:].reshape(n_seq, n_new, IDX_DIM), 1, 2),
                      ((0, 0), (0, 0), (0, PAGE - n_new)))
    scores = _sample_scores(page_table, jnp.swapaxes(cache_kidx, 2, 3), kinew_t, qi_s, w_s)
    col = np.arange(PAGE_ROWS)
    expand = jnp.asarray(col[None, :] // N_HEADS == np.arange(PAGE)[:, None], dtype=BF16)
    head_mask = jnp.asarray(np.where(np.arange(N_SROWS)[:, None] // n_new == col[None, :] % N_HEADS,
                                     0.0, 4.0 * NEG_BIG), dtype=F32)
    sel_exp = _sample_select(scores.reshape(n_seq * 8, past_len + PAGE), expand, (past_len + 8) * N_HEADS)
    sel_exp = sel_exp.reshape(n_seq, 8, (past_len + 8) * N_HEADS)
    new_rows = lambda a: jnp.pad(a.reshape(n_seq, n_new * N_HEADS, HEAD_DIM),
                                 ((0, 0), (0, (8 - n_new) * N_HEADS), (0, 0)))
    slope_rows = jnp.repeat(slopes, n_new).reshape(N_SROWS, 1)
    o_s = _sample_attention(page_table, hq(qv[t_pad:], HEAD_DIM),
                            cache_k.reshape(-1, HEAD_DIM), cache_v.reshape(-1, HEAD_DIM),
                            new_rows(k_all[t_pad:]), new_rows(v_all[t_pad:]), sel_exp, slope_rows, head_mask)
    o_s = o_s.reshape(n_seq, N_HEADS, n_new, HEAD_DIM).transpose(0, 2, 1, 3).reshape(n_s, d)

    pa = _matmul(y_rnn, w_proj_a[0].astype(BF16), F32, name="proj_a")
    merged = _merge(ot, o_s, ga, w_proj_b[0].astype(BF16), pa, gg)
    y_all = _out_proj(merged, w_out[0].astype(BF16), x_all, final_g)

    y_prompt = y_all[N_META:t_p][None]
    y_sample = y_all[t_pad:].reshape(n_seq, n_new, d)
    heads = lambda a, *lead: a.reshape(*lead, N_HEADS, HEAD_DIM)
    k_prompt = heads(k_all[:t_p], 1, 1, t_p)
    v_prompt = heads(v_all[:t_p], 1, 1, t_p)
    kidx_prompt = ki_all[:t_p][None, None]
    h_prompt = hs_p[t_p - 1][None, None]
    conv_prompt = xr[t_p - (CONV_W - 1):t_p][None, None]
    k_sample = heads(k_all[t_pad:], 1, n_seq, n_new)
    v_sample = heads(v_all[t_pad:], 1, n_seq, n_new)
    kidx_sample = ki_all[t_pad:].reshape(1, n_seq, n_new, IDX_DIM)
    h_sample = hs_s[n_new - 1][None]
    conv_sample = jnp.swapaxes(xs[n_new:], 0, 1)[None]
    return (y_prompt, y_sample, k_prompt, v_prompt, kidx_prompt, h_prompt, conv_prompt,
            k_sample, v_sample, kidx_sample, h_sample, conv_sample)
```

```python
import functools

import numpy as np
import jax
import jax.numpy as jnp
from jax import lax
from jax.experimental import pallas as pl
from jax.experimental.pallas import tpu as pltpu

F32 = jnp.float32
BF16 = jnp.bfloat16
I32 = jnp.int32

D_MODEL = 2048
N_HEADS = 16
HEAD_DIM = 128
N_IDX_HEADS = 16
IDX_DIM = 64
N_LRU_BLOCKS = 16
LRU_BLOCK = 128
CONV_W = 4
LRU_C = 8.0
N_META = 16
TOPK = 256
PAGE = 128
NORM_EPS = 1e-6
NEG_BIG = -1e30
INT_MIN = -(2 ** 31)

ROW_BLOCK = 256
Q_BLOCK = 256
S_CHUNK = 256
KEY_CHUNKS = (2048, 1024, S_CHUNK)
HEADS_PER_STEP = 2
LOG2E = 1.4426950408889634
VMEM_LIMIT = 56 * 1024 * 1024


def _cparams(*sem):
    return pltpu.CompilerParams(dimension_semantics=sem, vmem_limit_bytes=VMEM_LIMIT)


def _rmsnorm_kernel(x_ref, g_ref, o_ref):
    x = x_ref[...]
    y = x * lax.rsqrt(jnp.mean(x * x, axis=-1, keepdims=True) + NORM_EPS)
    o_ref[...] = (y * g_ref[...]).astype(o_ref.dtype)


def _rmsnorm_bf16(x, g):
    m, d = x.shape
    return pl.pallas_call(
        _rmsnorm_kernel,
        grid=(m // ROW_BLOCK,),
        in_specs=[pl.BlockSpec((ROW_BLOCK, d), lambda i: (i, 0)),
                  pl.BlockSpec((1, d), lambda i: (0, 0))],
        out_specs=pl.BlockSpec((ROW_BLOCK, d), lambda i: (i, 0)),
        out_shape=jax.ShapeDtypeStruct((m, d), BF16),
        compiler_params=_cparams("parallel"),
        name="rmsnorm_in",
    )(x, g.reshape(1, d))


def _mm_kernel(x_ref, w_ref, *o_refs, scale):
    acc = jnp.dot(x_ref[...], w_ref[...], preferred_element_type=F32)
    if scale is not None:
        acc = acc * scale
    for o_ref in o_refs:
        o_ref[...] = acc.astype(o_ref.dtype)


def _matmul(x, w, out_dtypes, *, cols=None, scale=None, tm=1280, tn=1024, name="matmul"):
    m, k = x.shape
    lo, hi = (0, w.shape[1]) if cols is None else cols
    n = hi - lo
    tn = min(tn, n)
    assert m % tm == 0 and n % tn == 0 and lo % tn == 0
    j0 = lo // tn
    single = not isinstance(out_dtypes, tuple)
    dts = (out_dtypes,) if single else out_dtypes
    outs = pl.pallas_call(
        functools.partial(_mm_kernel, scale=scale),
        grid=(n // tn, m // tm),
        in_specs=[pl.BlockSpec((tm, k), lambda j, i: (i, 0)),
                  pl.BlockSpec((k, tn), lambda j, i: (0, j0 + j))],
        out_specs=[pl.BlockSpec((tm, tn), lambda j, i: (i, j)) for _ in dts],
        out_shape=[jax.ShapeDtypeStruct((m, n), dt) for dt in dts],
        compiler_params=_cparams("parallel", "parallel"),
        name=name,
    )(x, w)
    return outs[0] if single else outs


def _softplus(x):
    return jnp.maximum(x, 0.0) + jnp.log1p(jnp.exp(-jnp.abs(x)))


def _sigmoid(x):
    return 0.5 * jnp.tanh(0.5 * x) + 0.5


def _lru_coeffs(xc, wa_ref, wx_ref, ba, bx, lam):
    rs, is_ = [], []
    for n in range(N_LRU_BLOCKS):
        sl = slice(n * LRU_BLOCK, (n + 1) * LRU_BLOCK)
        xb = xc[:, sl].astype(BF16)
        rs.append(jnp.dot(xb, wa_ref[n], preferred_element_type=F32))
        is_.append(jnp.dot(xb, wx_ref[n], preferred_element_type=F32))
    r = _sigmoid(jnp.concatenate(rs, axis=-1) + ba)
    i = _sigmoid(jnp.concatenate(is_, axis=-1) + bx)
    log_a = (-LRU_C * _softplus(-lam)) * r
    a = jnp.exp(log_a)
    u = jnp.sqrt(1.0 - a * a) * (i * xc)
    return a, u


def _lru_prompt_kernel(xr_ref, gr_ref, cw_ref, cb_ref, wa_ref, wx_ref, ba_ref, bx_ref, lam_ref,
                       hs_ref, y_ref, xp_scr, a_scr, u_scr, h_scr):
    tb = ROW_BLOCK

    @pl.when(pl.program_id(0) == 0)
    def _():
        xp_scr[0:8, :] = jnp.zeros((8, D_MODEL), F32)
        h_scr[...] = jnp.zeros((8, D_MODEL), F32)

    x = xr_ref[...]
    xp_scr[8:8 + tb, :] = x
    xc = cb_ref[...] + xp_scr[5:5 + tb, :] * cw_ref[0:1, :]
    xc = xc + xp_scr[6:6 + tb, :] * cw_ref[1:2, :]
    xc = xc + xp_scr[7:7 + tb, :] * cw_ref[2:3, :]
    xc = xc + x * cw_ref[3:4, :]
    xp_scr[0:8, :] = x[tb - 8:tb, :]

    a, u = _lru_coeffs(xc, wa_ref, wx_ref, ba_ref[...], bx_ref[...], lam_ref[...])
    a_scr[...] = a
    u_scr[...] = u

    def group(g, h):
        r0 = pl.multiple_of(g * 8, 8)
        a8 = a_scr[pl.ds(r0, 8), :]
        u8 = u_scr[pl.ds(r0, 8), :]
        rows = []
        for r in range(8):
            h = a8[r:r + 1, :] * h + u8[r:r + 1, :]
            rows.append(h)
        hs_ref[pl.ds(r0, 8), :] = jnp.concatenate(rows, axis=0)
        return h

    h_last = lax.fori_loop(0, tb // 8, group, h_scr[0:1, :])
    h_scr[0:1, :] = h_last

    g = gr_ref[...]
    y_ref[...] = (hs_ref[...] * (g * _sigmoid(g))).astype(y_ref.dtype)


def _lru_prompt(rg, cw, cb, wa, wx, ba, bx, lam, n_rows):
    d = D_MODEL
    row = lambda i: (i, 0)
    const2 = lambda i: (0, 0)
    const3 = lambda i: (0, 0, 0)
    return pl.pallas_call(
        _lru_prompt_kernel,
        grid=(n_rows // ROW_BLOCK,),
        in_specs=[pl.BlockSpec((ROW_BLOCK, d), row), pl.BlockSpec((ROW_BLOCK, d), lambda i: (i, 1)),
                  pl.BlockSpec((CONV_W, d), const2), pl.BlockSpec((1, d), const2),
                  pl.BlockSpec((N_LRU_BLOCKS, LRU_BLOCK, LRU_BLOCK), const3),
                  pl.BlockSpec((N_LRU_BLOCKS, LRU_BLOCK, LRU_BLOCK), const3),
                  pl.BlockSpec((1, d), const2), pl.BlockSpec((1, d), const2), pl.BlockSpec((1, d), const2)],
        out_specs=[pl.BlockSpec((ROW_BLOCK, d), row), pl.BlockSpec((ROW_BLOCK, d), row)],
        out_shape=[jax.ShapeDtypeStruct((n_rows, d), F32), jax.ShapeDtypeStruct((n_rows, d), BF16)],
        scratch_shapes=[pltpu.VMEM((8 + ROW_BLOCK, d), F32), pltpu.VMEM((ROW_BLOCK, d), F32),
                        pltpu.VMEM((ROW_BLOCK, d), F32), pltpu.VMEM((8, d), F32)],
        compiler_params=_cparams("arbitrary"),
        name="lru_prompt",
    )(rg, rg, cw, cb, wa, wx, ba, bx, lam)


def _lru_sample_kernel(xs_ref, gr_ref, h0_ref, cw_ref, cb_ref, wa_ref, wx_ref, ba_ref, bx_ref, lam_ref,
                       hs_ref, y_ref):
    n_t = hs_ref.shape[0]
    h = h0_ref[...]
    for t in range(n_t):
        xc = cb_ref[...] + xs_ref[t] * cw_ref[0:1, :]
        for j in range(1, CONV_W):
            xc = xc + xs_ref[t + j] * cw_ref[j:j + 1, :]
        a, u = _lru_coeffs(xc, wa_ref, wx_ref, ba_ref[...], bx_ref[...], lam_ref[...])
        h = a * h + u
        hs_ref[t] = h
        g = gr_ref[t]
        y_ref[t] = (h * (g * _sigmoid(g))).astype(y_ref.dtype)


def _lru_sample(xs, gr, h0, cw, cb, wa, wx, ba, bx, lam):
    n_t, b, d = gr.shape
    return pl.pallas_call(
        _lru_sample_kernel,
        out_shape=[jax.ShapeDtypeStruct((n_t, b, d), F32), jax.ShapeDtypeStruct((n_t, b, d), BF16)],
        compiler_params=pltpu.CompilerParams(vmem_limit_bytes=VMEM_LIMIT),
        name="lru_sample",
    )(xs, gr, h0, cw, cb, wa, wx, ba, bx, lam)


LOWEST_KEY = INT_MIN + 0x00800000
TIE_ROW_BITS = 14
TIE_ROW_LIMIT = 2 ** TIE_ROW_BITS - 1


def _key_to_f32(key):
    return pltpu.bitcast(key ^ ((key >> 31) & 0x7FFFFFFF), F32)


def _bisect_threshold(count_ge, zero):
    c0 = count_ge(jnp.zeros(zero.shape, F32))
    thr = jnp.where(c0 >= TOPK, zero, zero + INT_MIN)
    cnt = jnp.where(c0 >= TOPK, c0, 0.0)

    def body(p, carry):
        thr, cnt = carry
        cand = thr + jnp.left_shift(jnp.int32(1), 30 - p)
        c = count_ge(_key_to_f32(cand))
        return jnp.where(c >= TOPK, cand, thr), jnp.where(c >= TOPK, c, cnt)

    thr, cnt = lax.fori_loop(0, 31, body, (thr, cnt))
    return _key_to_f32(jnp.maximum(thr, LOWEST_KEY)), cnt


def _for_key_rows(n_small, body, carry):
    done = 0
    for size in KEY_CHUNKS:
        n_this = (n_small * S_CHUNK - done) // size
        carry = lax.fori_loop(
            0, n_this, lambda c, x, d=done, sz=size: body(pl.multiple_of(d + c * sz, sz), sz, x), carry)
        done = done + n_this * size
    return carry


def _prompt_attn_kernel(qit_ref, wt_ref, ki_ref, qt_ref, k_ref, vt_ref, slope_ref, o_ref, sel_scr, s_scr,
                        tie_scr):
    i = pl.program_id(0)
    h = pl.program_id(1)
    hps = HEADS_PER_STEP
    n_chunks = i + 1
    t0 = i * Q_BLOCK
    sc, qb = S_CHUNK, Q_BLOCK

    @pl.when(h == 0)
    def _select():
        def score_chunk(c, carry):
            r0 = pl.multiple_of(c * sc, sc)
            kc = ki_ref[pl.ds(r0, sc), :]
            acc = jnp.zeros((sc, qb), F32)
            for hh in range(N_IDX_HEADS):
                s = jnp.dot(kc, qit_ref[hh * IDX_DIM:(hh + 1) * IDX_DIM, :], preferred_element_type=F32)
                acc = acc + jnp.maximum(s, 0.0) * wt_ref[hh:hh + 1, :]
            row = r0 + lax.broadcasted_iota(I32, (sc, qb), 0)
            col = t0 + lax.broadcasted_iota(I32, (sc, qb), 1)
            sel_scr[pl.ds(r0, sc), :] = jnp.where(row <= col, acc, -jnp.inf)
            return carry

        lax.fori_loop(0, n_chunks, score_chunk, 0)

        def count_ge(cand):
            def body(r0, rows, cnt):
                ind = jnp.where(sel_scr[pl.ds(r0, rows), :] >= cand, 1.0, 0.0)
                return cnt + ind.reshape(rows // 8, 8, qb).sum(axis=0)

            cnt8 = _for_key_rows(n_chunks, body, jnp.zeros((8, qb), F32))
            return cnt8.sum(axis=0, keepdims=True)

        thr, n_ge = _bisect_threshold(count_ge, jnp.zeros((1, qb), I32))

        def count_rows(indicator):
            def body(r0, rows, cnt):
                row = r0 + lax.broadcasted_iota(I32, (rows, qb), 0)
                ind = indicator(sel_scr[pl.ds(r0, rows), :], row)
                return cnt + ind.reshape(rows // 8, 8, qb).sum(axis=0)

            return _for_key_rows(n_chunks, body, jnp.zeros((8, qb), F32)).sum(axis=0, keepdims=True)

        tie_scr[...] = jnp.full(tie_scr.shape, TIE_ROW_LIMIT, I32)

        @pl.when(jnp.max(n_ge) > TOPK)
        def _():
            need = TOPK - count_rows(lambda x, row: jnp.where(x > thr, 1.0, 0.0))

            def body(p, last):
                cand = last + jnp.left_shift(jnp.int32(1), TIE_ROW_BITS - 1 - p)
                below = count_rows(lambda x, row: jnp.where(x == thr, jnp.where(row < cand, 1.0, 0.0), 0.0))
                return jnp.where(below < need, cand, last)

            tie_scr[0:1, :] = lax.fori_loop(0, TIE_ROW_BITS, body, jnp.zeros((1, qb), I32))

        last = tie_scr[0:1, :]

        def mark(r0, rows, carry):
            row = r0 + lax.broadcasted_iota(I32, (rows, qb), 0)
            col = t0 + lax.broadcasted_iota(I32, (rows, qb), 1)
            x = sel_scr[pl.ds(r0, rows), :]
            dist = (row - col).astype(F32)
            sel_scr[pl.ds(r0, rows), :] = jnp.where(
                x == thr, jnp.where(row <= last, dist, NEG_BIG), jnp.where(x > thr, dist, NEG_BIG))
            return carry

        _for_key_rows(n_chunks, mark, 0)

    qh = [qt_ref[u * HEAD_DIM:(u + 1) * HEAD_DIM, :] for u in range(hps)]
    slope = [slope_ref[u, 0:1, :] for u in range(hps)]

    def logits(r0, rows, mx):
        kk = k_ref[pl.ds(r0, rows), :]
        nd = sel_scr[pl.ds(r0, rows), :]
        out = []
        for u in range(hps):
            s = jnp.dot(kk[:, u * HEAD_DIM:(u + 1) * HEAD_DIM], qh[u], preferred_element_type=F32)
            s = s + nd * slope[u]
            s_scr[u, pl.ds(r0, rows), :] = s
            out.append(jnp.maximum(mx[u], s.reshape(rows // 8, 8, qb).max(axis=0)))
        return tuple(out)

    mx8 = _for_key_rows(n_chunks, logits, tuple(jnp.full((8, qb), -3e38, F32) for _ in range(hps)))
    m = [x.max(axis=0, keepdims=True) for x in mx8]

    def weigh(r0, rows, carry):
        c0 = r0 // sc
        out = []
        for u in range(hps):
            l8, acc = carry[u]
            p = jnp.exp2(s_scr[u, pl.ds(r0, rows), :] - m[u])
            l8 = l8 + p.reshape(rows // 8, 8, qb).sum(axis=0)
            v = jnp.concatenate([vt_ref[u, c0 + j] for j in range(rows // sc)], axis=1)
            acc = acc + jnp.dot(v, p.astype(BF16), preferred_element_type=F32)
            out.append((l8, acc))
        return tuple(out)

    zero = (jnp.zeros((8, qb), F32), jnp.zeros((HEAD_DIM, qb), F32))
    res = _for_key_rows(n_chunks, weigh, tuple(zero for _ in range(hps)))
    for u in range(hps):
        l8, acc = res[u]
        o_ref[:, u * HEAD_DIM:(u + 1) * HEAD_DIM] = (acc / l8.sum(axis=0, keepdims=True)).T


def _prompt_attention(qit, wt, ki, qt, kv_bf, vt4, slopes, t_pad):
    n_blk = t_pad // Q_BLOCK
    hps = HEADS_PER_STEP
    assert t_pad <= TIE_ROW_LIMIT and t_pad % KEY_CHUNKS[-1] == 0
    return pl.pallas_call(
        _prompt_attn_kernel,
        grid=(n_blk, N_HEADS // hps),
        in_specs=[pl.BlockSpec((N_IDX_HEADS * IDX_DIM, Q_BLOCK), lambda i, h: (0, i)),
                  pl.BlockSpec((N_IDX_HEADS, Q_BLOCK), lambda i, h: (0, i)),
                  pl.BlockSpec((t_pad, IDX_DIM), lambda i, h: (0, 0)),
                  pl.BlockSpec((hps * HEAD_DIM, Q_BLOCK), lambda i, h: (h, i)),
                  pl.BlockSpec((t_pad, hps * HEAD_DIM), lambda i, h: (0, h)),
                  pl.BlockSpec((hps, t_pad // S_CHUNK, HEAD_DIM, S_CHUNK), lambda i, h: (h, 0, 0, 0)),
                  pl.BlockSpec((hps, 8, Q_BLOCK), lambda i, h: (h, 0, 0))],
        out_specs=pl.BlockSpec((Q_BLOCK, hps * HEAD_DIM), lambda i, h: (i, h)),
        out_shape=jax.ShapeDtypeStruct((t_pad, N_HEADS * HEAD_DIM), F32),
        scratch_shapes=[pltpu.VMEM((t_pad, Q_BLOCK), F32), pltpu.VMEM((hps, t_pad, Q_BLOCK), F32),
                        pltpu.VMEM((8, Q_BLOCK), I32)],
        compiler_params=_cparams("arbitrary", "arbitrary"),
        name="prompt_attention",
    )(qit, wt, ki, qt, kv_bf, vt4, slopes)


N_SROWS = 4 * N_HEADS
PAGE_ROWS = PAGE * N_HEADS
PAGES_PER_STEP = 8
PAGES_PER_GROUP = 4


def _sample_score_kernel(pt_ref, *refs, n_pages):
    kidx_refs = refs[:n_pages]
    kinew_ref, qi_ref, w_ref, o_ref = refs[n_pages:]
    qi = qi_ref[...]
    w = w_ref[...]

    def page_scores(page):
        s = jnp.dot(qi, page.astype(BF16), preferred_element_type=F32)
        x = jnp.maximum(s, 0.0) * w
        x8 = x.reshape(N_IDX_HEADS // 2, 8, PAGE).sum(axis=0)
        return x8 + pltpu.roll(x8, 4, 0)

    for p in range(n_pages):
        o_ref[:, p * PAGE:(p + 1) * PAGE] = page_scores(kidx_refs[p][...])
    slot = lax.broadcasted_iota(I32, (8, PAGE), 1)
    q = lax.broadcasted_iota(I32, (8, PAGE), 0) % 4
    o_ref[:, n_pages * PAGE:] = jnp.where(slot <= q, page_scores(kinew_ref[...]), NEG_BIG)


def _sample_scores(page_table, cache_kidx_t, kinew_t, qi_s, w_s):
    b, n_pages = page_table.shape

    def page_index(s, pt, *, p):
        return (0, pt[s, p], 0, 0)

    grid_spec = pltpu.PrefetchScalarGridSpec(
        num_scalar_prefetch=1,
        grid=(b,),
        in_specs=[pl.BlockSpec((None, None, IDX_DIM, PAGE), functools.partial(page_index, p=p))
                  for p in range(n_pages)]
        + [pl.BlockSpec((None, IDX_DIM, PAGE), lambda s, pt: (s, 0, 0)),
           pl.BlockSpec((None, N_SROWS, IDX_DIM), lambda s, pt: (s, 0, 0)),
           pl.BlockSpec((None, N_SROWS, PAGE), lambda s, pt: (s, 0, 0))],
        out_specs=pl.BlockSpec((None, 8, (n_pages + 1) * PAGE), lambda s, pt: (s, 0, 0)),
    )
    return pl.pallas_call(
        functools.partial(_sample_score_kernel, n_pages=n_pages),
        grid_spec=grid_spec,
        out_shape=jax.ShapeDtypeStruct((b, 8, (n_pages + 1) * PAGE), F32),
        compiler_params=_cparams("arbitrary"),
        name="sample_scores",
    )(page_table, *([cache_kidx_t] * n_pages), kinew_t, qi_s, w_s)


def _sample_select_kernel(s_ref, e_ref, o_ref, tie_scr):
    sc = s_ref[...]
    rows, width = sc.shape

    def count_ge(cand):
        return jnp.sum(jnp.where(sc >= cand, 1.0, 0.0), axis=1, keepdims=True)

    thr, n_ge = _bisect_threshold(count_ge, jnp.zeros((rows, 1), I32))
    pos = lax.broadcasted_iota(I32, (rows, width), 1)
    tie_scr[...] = jnp.full(tie_scr.shape, TIE_ROW_LIMIT, I32)

    @pl.when(jnp.max(n_ge) > TOPK)
    def _():
        need = TOPK - jnp.sum(jnp.where(sc > thr, 1.0, 0.0), axis=1, keepdims=True)

        def tie_body(p, last):
            cand = last + jnp.left_shift(jnp.int32(1), TIE_ROW_BITS - 1 - p)
            below = jnp.sum(jnp.where(sc == thr, jnp.where(pos < cand, 1.0, 0.0), 0.0), axis=1, keepdims=True)
            return jnp.where(below < need, cand, last)

        tie_scr[...] = lax.fori_loop(0, TIE_ROW_BITS, tie_body, jnp.zeros((rows, 1), I32))

    last = tie_scr[...]
    sel = jnp.where(sc == thr, jnp.where(pos <= last, 1.0, 0.0), jnp.where(sc > thr, 1.0, 0.0)).astype(BF16)
    n_tiles = width // PAGE
    for t in range(n_tiles):
        x = jnp.dot(sel[:, t * PAGE:(t + 1) * PAGE], e_ref[...], preferred_element_type=F32)
        lo = t * PAGE_ROWS
        hi = min(lo + PAGE_ROWS, o_ref.shape[1])
        o_ref[:, lo:hi] = x[:, :hi - lo]


def _sample_select(scores, expand, out_width):
    rows, width = scores.shape
    rb = 128
    return pl.pallas_call(
        _sample_select_kernel,
        grid=(rows // rb,),
        in_specs=[pl.BlockSpec((rb, width), lambda i: (i, 0)),
                  pl.BlockSpec((PAGE, PAGE_ROWS), lambda i: (0, 0))],
        out_specs=pl.BlockSpec((rb, out_width), lambda i: (i, 0)),
        out_shape=jax.ShapeDtypeStruct((rows, out_width), F32),
        scratch_shapes=[pltpu.VMEM((rb, 1), I32)],
        compiler_params=_cparams("parallel"),
        name="sample_select",
    )(scores, expand)


def _sample_attn_kernel(pt_ref, q_ref, *refs, n_steps, past_len):
    pps = PAGES_PER_STEP
    k_refs, v_refs = refs[:pps], refs[pps:2 * pps]
    kn_ref, vn_ref, selp_ref, seln_ref, slope_ref, hm_ref, o_ref, m_scr, l_scr, acc_scr = refs[2 * pps:]
    jj = pl.program_id(1)

    @pl.when(jj == 0)
    def _():
        m_scr[...] = jnp.full(m_scr.shape, -3e38, F32)
        l_scr[...] = jnp.zeros(l_scr.shape, F32)
        acc_scr[...] = jnp.zeros(acc_scr.shape, F32)

    def step(k_list, v_list, sel8, pos0):
        q = q_ref[...]
        s = jnp.concatenate(
            [lax.dot_general(q, k.astype(BF16), (((1,), (1,)), ((), ())), preferred_element_type=F32)
             for k in k_list], axis=1)
        width = s.shape[1]
        lane = lax.broadcasted_iota(I32, (8, width), 1)
        rq = lax.broadcasted_iota(I32, (8, width), 0) % 4
        dist = (pos0 - past_len + lane // N_HEADS - rq).astype(F32)
        hm = jnp.concatenate([hm_ref[...]] * (width // PAGE_ROWS), axis=1) if width > PAGE_ROWS \
            else hm_ref[:, :width]
        g = N_SROWS // 8
        x = s.reshape(g, 8, width) + slope_ref[...].reshape(g, 8, 1) * dist[None]
        x = jnp.where(sel8[None] > 0.5, x, NEG_BIG) + hm.reshape(g, 8, width)
        x = x.reshape(N_SROWS, width)
        m = m_scr[...]
        m_new = jnp.maximum(m, jnp.max(x, axis=1, keepdims=True))
        alpha = jnp.exp2(m - m_new)
        p = jnp.exp2(x - m_new)
        l_scr[...] = alpha * l_scr[...] + jnp.sum(p, axis=1, keepdims=True)
        pb = p.astype(BF16)
        pv = None
        off = 0
        for v in v_list:
            part = jnp.dot(pb[:, off:off + v.shape[0]], v.astype(BF16), preferred_element_type=F32)
            pv = part if pv is None else pv + part
            off += v.shape[0]
        acc_scr[...] = alpha * acc_scr[...] + pv
        m_scr[...] = m_new

    ppg = PAGES_PER_GROUP
    for g in range(pps // ppg):
        lanes = slice(g * ppg * PAGE_ROWS, (g + 1) * ppg * PAGE_ROWS)
        step([r[...] for r in k_refs[g * ppg:(g + 1) * ppg]], [r[...] for r in v_refs[g * ppg:(g + 1) * ppg]],
             selp_ref[:, lanes], (jj * pps + g * ppg) * PAGE)

    @pl.when(jj == n_steps - 1)
    def _():
        step([kn_ref[...]], [vn_ref[...]], seln_ref[...], past_len)
        o_ref[...] = acc_scr[...] / l_scr[...]


def _sample_attention(page_table, q_s, cache_k2, cache_v2, knew, vnew, sel_exp, slope_rows, head_mask):
    b, n_pages = page_table.shape
    pps = PAGES_PER_STEP
    assert n_pages % pps == 0 and pps % PAGES_PER_GROUP == 0
    n_steps = n_pages // pps
    nw = knew.shape[1]

    def page_index(s, jj, pt, *, u):
        return (pt[s, jj * pps + u], 0)

    page_specs = [pl.BlockSpec((PAGE_ROWS, HEAD_DIM), functools.partial(page_index, u=u)) for u in range(pps)]
    grid_spec = pltpu.PrefetchScalarGridSpec(
        num_scalar_prefetch=1,
        grid=(b, n_steps),
        in_specs=[pl.BlockSpec((None, N_SROWS, HEAD_DIM), lambda s, jj, pt: (s, 0, 0))]
        + page_specs + page_specs
        + [pl.BlockSpec((None, nw, HEAD_DIM), lambda s, jj, pt: (s, 0, 0)),
           pl.BlockSpec((None, nw, HEAD_DIM), lambda s, jj, pt: (s, 0, 0)),
           pl.BlockSpec((None, 8, pps * PAGE_ROWS), lambda s, jj, pt: (s, 0, jj)),
           pl.BlockSpec((None, 8, nw), lambda s, jj, pt: (s, 0, n_pages * PAGE_ROWS // nw)),
           pl.BlockSpec((N_SROWS, 1), lambda s, jj, pt: (0, 0)),
           pl.BlockSpec((N_SROWS, PAGE_ROWS), lambda s, jj, pt: (0, 0))],
        out_specs=pl.BlockSpec((None, N_SROWS, HEAD_DIM), lambda s, jj, pt: (s, 0, 0)),
        scratch_shapes=[pltpu.VMEM((N_SROWS, 1), F32), pltpu.VMEM((N_SROWS, 1), F32),
                        pltpu.VMEM((N_SROWS, HEAD_DIM), F32)],
    )
    return pl.pallas_call(
        functools.partial(_sample_attn_kernel, n_steps=n_steps, past_len=n_pages * PAGE),
        grid_spec=grid_spec,
        out_shape=jax.ShapeDtypeStruct((b, N_SROWS, HEAD_DIM), F32),
        compiler_params=_cparams("arbitrary", "arbitrary"),
        name="sample_attention",
    )(page_table, q_s, *([cache_k2] * pps), *([cache_v2] * pps), knew, vnew, sel_exp, sel_exp, slope_rows,
      head_mask)


def _merge_kernel(op_ref, os_ref, ga_ref, y_ref, wa_ref, wb_ref, g1_ref, g2_ref, out_ref, *, n_prompt_blocks):
    o = jnp.where(pl.program_id(0) < n_prompt_blocks, op_ref[...], os_ref[...])
    ga = ga_ref[...]
    og = (o * (ga * _sigmoid(ga))).astype(BF16)
    pa = jnp.dot(y_ref[...], wa_ref[...], preferred_element_type=F32)
    pb = jnp.dot(og, wb_ref[...], preferred_element_type=F32)
    merged = _sigmoid(g1_ref[...]) * pa + _sigmoid(g2_ref[...]) * pb
    out_ref[...] = merged.astype(out_ref.dtype)


def _merge(o_p, o_s, ga, y_rnn, w_pa, w_pb, gg):
    m, d = ga.shape
    npb = o_p.shape[0] // ROW_BLOCK
    row = lambda i: (i, 0)
    weight = pl.BlockSpec((d, d), lambda i: (0, 0), pipeline_mode=pl.Buffered(1))
    return pl.pallas_call(
        functools.partial(_merge_kernel, n_prompt_blocks=npb),
        grid=(m // ROW_BLOCK,),
        in_specs=[pl.BlockSpec((ROW_BLOCK, d), lambda i: (jnp.minimum(i, npb - 1), 0)),
                  pl.BlockSpec((ROW_BLOCK, d), lambda i: (jnp.maximum(i - npb, 0), 0)),
                  pl.BlockSpec((ROW_BLOCK, d), row), pl.BlockSpec((ROW_BLOCK, d), row),
                  weight, weight,
                  pl.BlockSpec((ROW_BLOCK, d), row), pl.BlockSpec((ROW_BLOCK, d), lambda i: (i, 1))],
        out_specs=pl.BlockSpec((ROW_BLOCK, d), row),
        out_shape=jax.ShapeDtypeStruct((m, d), BF16),
        compiler_params=_cparams("parallel"),
        name="merge",
    )(o_p, o_s, ga, y_rnn, w_pa, w_pb, gg, gg)


def _out_kernel(mg_ref, w_ref, x_ref, g_ref, y_ref):
    res = x_ref[...] + jnp.dot(mg_ref[...], w_ref[...], preferred_element_type=F32)
    y = res * lax.rsqrt(jnp.mean(res * res, axis=-1, keepdims=True) + NORM_EPS)
    y_ref[...] = y * g_ref[...]


def _out_proj(merged, w_out, x, g):
    m, d = x.shape
    row = lambda i: (i, 0)
    return pl.pallas_call(
        _out_kernel,
        grid=(m // ROW_BLOCK,),
        in_specs=[pl.BlockSpec((ROW_BLOCK, d), row), pl.BlockSpec((d, d), lambda i: (0, 0)),
                  pl.BlockSpec((ROW_BLOCK, d), row), pl.BlockSpec((1, d), lambda i: (0, 0))],
        out_specs=pl.BlockSpec((ROW_BLOCK, d), row),
        out_shape=jax.ShapeDtypeStruct((m, d), F32),
        compiler_params=_cparams("parallel"),
        name="out_proj",
    )(merged, w_out, x, g.reshape(1, d))


def kernel(x_prompt, x_sample, cache_k, cache_v, cache_kidx, state_h, state_conv, page_table, meta_tokens,
           norm_g, w_in, conv_w, conv_b, lru_wa, lru_ba, lru_wx, lru_bx, lru_lambda, w_proj_a, w_proj_b,
           w_out, final_g):
    assert x_prompt.shape[0] == 1 and norm_g.shape[0] == 1
    d = D_MODEL
    seq = x_prompt.shape[1]
    t_p = seq + N_META
    t_pad = -(-t_p // Q_BLOCK) * Q_BLOCK
    n_seq, n_new = x_sample.shape[:2]
    n_s = n_seq * n_new
    n_pages = page_table.shape[1]
    past_len = n_pages * PAGE
    assert n_new == 4 and n_s % ROW_BLOCK == 0
    m_all = t_pad + n_s

    x_all = jnp.concatenate([meta_tokens.astype(F32), x_prompt[0], jnp.zeros((t_pad - t_p, d), F32),
                             x_sample.reshape(n_s, d)], axis=0)
    xn = _rmsnorm_bf16(x_all, norm_g[0])

    wb = w_in[0].astype(BF16)
    cuts = np.cumsum([d, d, d, d, d, d, N_IDX_HEADS * IDX_DIM, IDX_DIM, N_IDX_HEADS, d, d]).tolist()
    rg = _matmul(xn, wb, F32, cols=(0, cuts[1]), name="proj_rnn")
    qv = _matmul(xn, wb, BF16, cols=(cuts[1], cuts[2]), scale=LOG2E * HEAD_DIM ** -0.5, name="proj_q")
    kv, kv_bf = _matmul(xn, wb, (F32, BF16), cols=(cuts[2], cuts[4]), name="proj_kv")
    ga = _matmul(xn, wb, F32, cols=(cuts[4], cuts[5]), name="proj_ga")
    qi = _matmul(xn, wb, BF16, cols=(cuts[5], cuts[6]), scale=IDX_DIM ** -0.5, name="proj_qi")
    w_kw = jnp.pad(wb[:, cuts[6]:cuts[8]], ((0, 0), (0, 128 - (cuts[8] - cuts[6]))))
    kw = _matmul(xn, w_kw, F32, name="proj_kiwi")
    gg = _matmul(xn, wb[:, cuts[8]:], F32, name="proj_gates")

    xr, gr = rg[:, :d], rg[:, d:]
    k_all, v_all = kv[:, :d], kv[:, d:]
    ki_all = kw[:, :IDX_DIM]
    wi_all = kw[:, IDX_DIM:IDX_DIM + N_IDX_HEADS] * (N_IDX_HEADS ** -0.5)

    cw, cb = conv_w[0], conv_b[0].reshape(1, d)
    wa, wx = lru_wa[0].astype(BF16), lru_wx[0].astype(BF16)
    ba, bx, lam = lru_ba[0].reshape(1, d), lru_bx[0].reshape(1, d), lru_lambda[0].reshape(1, d)
    hs_p, y_p = _lru_prompt(rg, cw, cb, wa, wx, ba, bx, lam, t_pad)
    xr_s = jnp.swapaxes(xr[t_pad:].reshape(n_seq, n_new, d), 0, 1)
    gr_s = jnp.swapaxes(gr[t_pad:].reshape(n_seq, n_new, d), 0, 1)
    xs = jnp.concatenate([jnp.swapaxes(state_conv[0], 0, 1), xr_s], axis=0)
    hs_s, y_s = _lru_sample(xs, gr_s, state_h[0], cw, cb, wa, wx, ba, bx, lam)
    y_rnn = jnp.concatenate([y_p, jnp.swapaxes(y_s, 0, 1).reshape(n_s, d)], axis=0)

    slopes = jnp.asarray(LOG2E * 2.0 ** (-8.0 * np.arange(1, N_HEADS + 1) / N_HEADS), dtype=F32)
    qit = qi.T
    wt = wi_all.T
    qt = qv.T
    vt4 = kv_bf[:t_pad, d:].reshape(t_pad // S_CHUNK, S_CHUNK, N_HEADS, HEAD_DIM).transpose(2, 0, 3, 1)
    slope_b = jnp.broadcast_to(slopes[:, None, None], (N_HEADS, 8, Q_BLOCK))
    o_p = _prompt_attention(qit, wt, ki_all.astype(BF16), qt, kv_bf, vt4, slope_b, t_pad)

    hq = lambda a, width: a.reshape(n_seq, n_new, N_HEADS, width).transpose(0, 2, 1, 3).reshape(
        n_seq, N_HEADS * n_new, width)
    qi_s = hq(qi[t_pad:], IDX_DIM)
    w_s = jnp.broadcast_to(
        wi_all[t_pad:].reshape(n_seq, n_new, N_IDX_HEADS).transpose(0, 2, 1).reshape(n_seq, -1, 1),
        (n_seq, N_IDX_HEADS * n_new, PAGE))
    kinew_t = jnp.pad(jnp.swapaxes(ki_all[t_pad:].reshape(n_seq, n_new, IDX_DIM), 1, 2),
                      ((0, 0), (0, 0), (0, PAGE - n_new)))
    scores = _sample_scores(page_table, jnp.swapaxes(cache_kidx, 2, 3), kinew_t, qi_s, w_s)
    col = np.arange(PAGE_ROWS)
    expand = jnp.asarray(col[None, :] // N_HEADS == np.arange(PAGE)[:, None], dtype=BF16)
    head_mask = jnp.asarray(np.where(np.arange(N_SROWS)[:, None] // n_new == col[None, :] % N_HEADS,
                                     0.0, 4.0 * NEG_BIG), dtype=F32)
    sel_exp = _sample_select(scores.reshape(n_seq * 8, past_len + PAGE), expand, (past_len + 8) * N_HEADS)
    sel_exp = sel_exp.reshape(n_seq, 8, (past_len + 8) * N_HEADS)
    new_rows = lambda a: jnp.pad(a.reshape(n_seq, n_new * N_HEADS, HEAD_DIM),
                                 ((0, 0), (0, (8 - n_new) * N_HEADS), (0, 0)))
    slope_rows = jnp.repeat(slopes, n_new).reshape(N_SROWS, 1)
    o_s = _sample_attention(page_table, hq(qv[t_pad:], HEAD_DIM),
                            cache_k.reshape(-1, HEAD_DIM), cache_v.reshape(-1, HEAD_DIM),
                            new_rows(k_all[t_pad:]), new_rows(v_all[t_pad:]), sel_exp, slope_rows, head_mask)
    o_s = o_s.reshape(n_seq, N_HEADS, n_new, HEAD_DIM).transpose(0, 2, 1, 3).reshape(n_s, d)

    merged = _merge(o_p, o_s, ga, y_rnn, w_proj_a[0].astype(BF16), w_proj_b[0].astype(BF16), gg)
    y_all = _out_proj(merged, w_out[0].astype(BF16), x_all, final_g)

    y_prompt = y_all[N_META:t_p][None]
    y_sample = y_all[t_pad:].reshape(n_seq, n_new, d)
    heads = lambda a, *lead: a.reshape(*lead, N_HEADS, HEAD_DIM)
    k_prompt = heads(k_all[:t_p], 1, 1, t_p)
    v_prompt = heads(v_all[:t_p], 1, 1, t_p)
    kidx_prompt = ki_all[:t_p][None, None]
    h_prompt = hs_p[t_p - 1][None, None]
    conv_prompt = xr[t_p - (CONV_W - 1):t_p][None, None]
    k_sample = heads(k_all[t_pad:], 1, n_seq, n_new)
    v_sample = heads(v_all[t_pad:], 1, n_seq, n_new)
    kidx_sample = ki_all[t_pad:].reshape(1, n_seq, n_new, IDX_DIM)
    h_sample = hs_s[n_new - 1][None]
    conv_sample = jnp.swapaxes(xs[n_new:], 0, 1)[None]
    return (y_prompt, y_sample, k_prompt, v_prompt, kidx_prompt, h_prompt, conv_prompt,
            k_sample, v_sample, kidx_sample, h_sample, conv_sample)
```

```python
import functools

import numpy as np
import jax
import jax.numpy as jnp
from jax import lax
from jax.experimental import pallas as pl
from jax.experimental.pallas import tpu as pltpu

F32 = jnp.float32
BF16 = jnp.bfloat16
I32 = jnp.int32

D_MODEL = 2048
N_HEADS = 16
HEAD_DIM = 128
N_IDX_HEADS = 16
IDX_DIM = 64
N_LRU_BLOCKS = 16
LRU_BLOCK = 128
CONV_W = 4
LRU_C = 8.0
N_META = 16
TOPK = 256
PAGE = 128
NORM_EPS = 1e-6
NEG_BIG = -1e30
INT_MIN = -(2 ** 31)

ROW_BLOCK = 256
Q_BLOCK = 256
S_CHUNK = 256
KEY_CHUNKS = (2048, 1024, S_CHUNK)
HEADS_PER_STEP = 2
LOG2E = 1.4426950408889634
VMEM_LIMIT = 56 * 1024 * 1024
MERGE_VMEM_LIMIT = 60 * 1024 * 1024


def _cparams(*sem):
    return pltpu.CompilerParams(dimension_semantics=sem, vmem_limit_bytes=VMEM_LIMIT)


def _rmsnorm_kernel(x_ref, g_ref, o_ref):
    x = x_ref[...]
    y = x * lax.rsqrt(jnp.mean(x * x, axis=-1, keepdims=True) + NORM_EPS)
    o_ref[...] = (y * g_ref[...]).astype(o_ref.dtype)


def _rmsnorm_bf16(x, g):
    m, d = x.shape
    return pl.pallas_call(
        _rmsnorm_kernel,
        grid=(m // ROW_BLOCK,),
        in_specs=[pl.BlockSpec((ROW_BLOCK, d), lambda i: (i, 0)),
                  pl.BlockSpec((1, d), lambda i: (0, 0))],
        out_specs=pl.BlockSpec((ROW_BLOCK, d), lambda i: (i, 0)),
        out_shape=jax.ShapeDtypeStruct((m, d), BF16),
        compiler_params=_cparams("parallel"),
        name="rmsnorm_in",
    )(x, g.reshape(1, d))


def _mm_kernel(x_ref, w_ref, *o_refs, scale):
    acc = jnp.dot(x_ref[...], w_ref[...], preferred_element_type=F32)
    if scale is not None:
        acc = acc * scale
    for o_ref in o_refs:
        o_ref[...] = acc.astype(o_ref.dtype)


def _matmul(x, w, out_dtypes, *, cols=None, scale=None, tm=1280, tn=1024, name="matmul"):
    m, k = x.shape
    lo, hi = (0, w.shape[1]) if cols is None else cols
    n = hi - lo
    tn = min(tn, n)
    assert m % tm == 0 and n % tn == 0 and lo % tn == 0
    j0 = lo // tn
    single = not isinstance(out_dtypes, tuple)
    dts = (out_dtypes,) if single else out_dtypes
    outs = pl.pallas_call(
        functools.partial(_mm_kernel, scale=scale),
        grid=(n // tn, m // tm),
        in_specs=[pl.BlockSpec((tm, k), lambda j, i: (i, 0)),
                  pl.BlockSpec((k, tn), lambda j, i: (0, j0 + j))],
        out_specs=[pl.BlockSpec((tm, tn), lambda j, i: (i, j)) for _ in dts],
        out_shape=[jax.ShapeDtypeStruct((m, n), dt) for dt in dts],
        compiler_params=_cparams("parallel", "parallel"),
        name=name,
    )(x, w)
    return outs[0] if single else outs


def _softplus(x):
    return jnp.maximum(x, 0.0) + jnp.log1p(jnp.exp(-jnp.abs(x)))


def _sigmoid(x):
    return 0.5 * jnp.tanh(0.5 * x) + 0.5


def _lru_coeffs(xc, wa_ref, wx_ref, ba, bx, lam):
    rs, is_ = [], []
    for n in range(N_LRU_BLOCKS):
        sl = slice(n * LRU_BLOCK, (n + 1) * LRU_BLOCK)
        xb = xc[:, sl].astype(BF16)
        rs.append(jnp.dot(xb, wa_ref[n], preferred_element_type=F32))
        is_.append(jnp.dot(xb, wx_ref[n], preferred_element_type=F32))
    r = _sigmoid(jnp.concatenate(rs, axis=-1) + ba)
    i = _sigmoid(jnp.concatenate(is_, axis=-1) + bx)
    log_a = (-LRU_C * _softplus(-lam)) * r
    a = jnp.exp(log_a)
    u = jnp.sqrt(1.0 - a * a) * (i * xc)
    return a, u


def _lru_prompt_kernel(xr_ref, gr_ref, cw_ref, cb_ref, wa_ref, wx_ref, ba_ref, bx_ref, lam_ref,
                       hs_ref, y_ref, xp_scr, a_scr, u_scr, h_scr):
    tb = ROW_BLOCK

    @pl.when(pl.program_id(0) == 0)
    def _():
        xp_scr[0:8, :] = jnp.zeros((8, D_MODEL), F32)
        h_scr[...] = jnp.zeros((8, D_MODEL), F32)

    x = xr_ref[...]
    xp_scr[8:8 + tb, :] = x
    xc = cb_ref[...] + xp_scr[5:5 + tb, :] * cw_ref[0:1, :]
    xc = xc + xp_scr[6:6 + tb, :] * cw_ref[1:2, :]
    xc = xc + xp_scr[7:7 + tb, :] * cw_ref[2:3, :]
    xc = xc + x * cw_ref[3:4, :]
    xp_scr[0:8, :] = x[tb - 8:tb, :]

    a, u = _lru_coeffs(xc, wa_ref, wx_ref, ba_ref[...], bx_ref[...], lam_ref[...])
    a_scr[...] = a
    u_scr[...] = u

    def group(g, h):
        r0 = pl.multiple_of(g * 8, 8)
        a8 = a_scr[pl.ds(r0, 8), :]
        u8 = u_scr[pl.ds(r0, 8), :]
        rows = []
        for r in range(8):
            h = a8[r:r + 1, :] * h + u8[r:r + 1, :]
            rows.append(h)
        hs_ref[pl.ds(r0, 8), :] = jnp.concatenate(rows, axis=0)
        return h

    h_last = lax.fori_loop(0, tb // 8, group, h_scr[0:1, :])
    h_scr[0:1, :] = h_last

    g = gr_ref[...]
    y_ref[...] = (hs_ref[...] * (g * _sigmoid(g))).astype(y_ref.dtype)


def _lru_prompt(rg, cw, cb, wa, wx, ba, bx, lam, n_rows):
    d = D_MODEL
    row = lambda i: (i, 0)
    const2 = lambda i: (0, 0)
    const3 = lambda i: (0, 0, 0)
    return pl.pallas_call(
        _lru_prompt_kernel,
        grid=(n_rows // ROW_BLOCK,),
        in_specs=[pl.BlockSpec((ROW_BLOCK, d), row), pl.BlockSpec((ROW_BLOCK, d), lambda i: (i, 1)),
                  pl.BlockSpec((CONV_W, d), const2), pl.BlockSpec((1, d), const2),
                  pl.BlockSpec((N_LRU_BLOCKS, LRU_BLOCK, LRU_BLOCK), const3),
                  pl.BlockSpec((N_LRU_BLOCKS, LRU_BLOCK, LRU_BLOCK), const3),
                  pl.BlockSpec((1, d), const2), pl.BlockSpec((1, d), const2), pl.BlockSpec((1, d), const2)],
        out_specs=[pl.BlockSpec((ROW_BLOCK, d), row), pl.BlockSpec((ROW_BLOCK, d), row)],
        out_shape=[jax.ShapeDtypeStruct((n_rows, d), F32), jax.ShapeDtypeStruct((n_rows, d), BF16)],
        scratch_shapes=[pltpu.VMEM((8 + ROW_BLOCK, d), F32), pltpu.VMEM((ROW_BLOCK, d), F32),
                        pltpu.VMEM((ROW_BLOCK, d), F32), pltpu.VMEM((8, d), F32)],
        compiler_params=_cparams("arbitrary"),
        name="lru_prompt",
    )(rg, rg, cw, cb, wa, wx, ba, bx, lam)


def _lru_sample_kernel(xs_ref, gr_ref, h0_ref, cw_ref, cb_ref, wa_ref, wx_ref, ba_ref, bx_ref, lam_ref,
                       hs_ref, y_ref):
    n_t = hs_ref.shape[0]
    h = h0_ref[...]
    for t in range(n_t):
        xc = cb_ref[...] + xs_ref[t] * cw_ref[0:1, :]
        for j in range(1, CONV_W):
            xc = xc + xs_ref[t + j] * cw_ref[j:j + 1, :]
        a, u = _lru_coeffs(xc, wa_ref, wx_ref, ba_ref[...], bx_ref[...], lam_ref[...])
        h = a * h + u
        hs_ref[t] = h
        g = gr_ref[t]
        y_ref[t] = (h * (g * _sigmoid(g))).astype(y_ref.dtype)


def _lru_sample(xs, gr, h0, cw, cb, wa, wx, ba, bx, lam):
    n_t, b, d = gr.shape
    return pl.pallas_call(
        _lru_sample_kernel,
        out_shape=[jax.ShapeDtypeStruct((n_t, b, d), F32), jax.ShapeDtypeStruct((n_t, b, d), BF16)],
        compiler_params=pltpu.CompilerParams(vmem_limit_bytes=VMEM_LIMIT),
        name="lru_sample",
    )(xs, gr, h0, cw, cb, wa, wx, ba, bx, lam)


LOWEST_KEY = INT_MIN + 0x00800000
TIE_ROW_BITS = 14
TIE_ROW_LIMIT = 2 ** TIE_ROW_BITS - 1


def _key_to_f32(key):
    return pltpu.bitcast(key ^ ((key >> 31) & 0x7FFFFFFF), F32)


def _bisect_threshold(count_ge, zero):
    c0 = count_ge(jnp.zeros(zero.shape, F32))
    thr = jnp.where(c0 >= TOPK, zero, zero + INT_MIN)
    cnt = jnp.where(c0 >= TOPK, c0, 0.0)

    def body(p, carry):
        thr, cnt = carry
        cand = thr + jnp.left_shift(jnp.int32(1), 30 - p)
        c = count_ge(_key_to_f32(cand))
        return jnp.where(c >= TOPK, cand, thr), jnp.where(c >= TOPK, c, cnt)

    thr, cnt = lax.fori_loop(0, 31, body, (thr, cnt))
    return _key_to_f32(jnp.maximum(thr, LOWEST_KEY)), cnt


def _for_key_rows(n_small, body, carry):
    done = 0
    for size in KEY_CHUNKS:
        n_this = (n_small * S_CHUNK - done) // size
        carry = lax.fori_loop(
            0, n_this, lambda c, x, d=done, sz=size: body(pl.multiple_of(d + c * sz, sz), sz, x), carry)
        done = done + n_this * size
    return carry


def _prompt_attn_kernel(qit_ref, wt_ref, ki_ref, qt_ref, k_ref, vt_ref, slope_ref, o_ref, sel_scr, s_scr,
                        tie_scr):
    i = pl.program_id(0)
    h = pl.program_id(1)
    hps = HEADS_PER_STEP
    n_chunks = i + 1
    t0 = i * Q_BLOCK
    sc, qb = S_CHUNK, Q_BLOCK

    @pl.when(h == 0)
    def _select():
        def score_chunk(c, carry):
            r0 = pl.multiple_of(c * sc, sc)
            kc = ki_ref[pl.ds(r0, sc), :]
            acc = jnp.zeros((sc, qb), F32)
            for hh in range(N_IDX_HEADS):
                s = jnp.dot(kc, qit_ref[hh * IDX_DIM:(hh + 1) * IDX_DIM, :], preferred_element_type=F32)
                acc = acc + jnp.maximum(s, 0.0) * wt_ref[hh:hh + 1, :]
            row = r0 + lax.broadcasted_iota(I32, (sc, qb), 0)
            col = t0 + lax.broadcasted_iota(I32, (sc, qb), 1)
            sel_scr[pl.ds(r0, sc), :] = jnp.where(row <= col, acc, -jnp.inf)
            return carry

        lax.fori_loop(0, n_chunks, score_chunk, 0)

        def count_ge(cand):
            def body(r0, rows, cnt):
                ind = jnp.where(sel_scr[pl.ds(r0, rows), :] >= cand, 1.0, 0.0)
                return cnt + ind.reshape(rows // 8, 8, qb).sum(axis=0)

            cnt8 = _for_key_rows(n_chunks, body, jnp.zeros((8, qb), F32))
            return cnt8.sum(axis=0, keepdims=True)

        thr, n_ge = _bisect_threshold(count_ge, jnp.zeros((1, qb), I32))

        def count_rows(indicator):
            def body(r0, rows, cnt):
                row = r0 + lax.broadcasted_iota(I32, (rows, qb), 0)
                ind = indicator(sel_scr[pl.ds(r0, rows), :], row)
                return cnt + ind.reshape(rows // 8, 8, qb).sum(axis=0)

            return _for_key_rows(n_chunks, body, jnp.zeros((8, qb), F32)).sum(axis=0, keepdims=True)

        tie_scr[...] = jnp.full(tie_scr.shape, TIE_ROW_LIMIT, I32)

        @pl.when(jnp.max(n_ge) > TOPK)
        def _():
            need = TOPK - count_rows(lambda x, row: jnp.where(x > thr, 1.0, 0.0))

            def body(p, last):
                cand = last + jnp.left_shift(jnp.int32(1), TIE_ROW_BITS - 1 - p)
                below = count_rows(lambda x, row: jnp.where(x == thr, jnp.where(row < cand, 1.0, 0.0), 0.0))
                return jnp.where(below < need, cand, last)

            tie_scr[0:1, :] = lax.fori_loop(0, TIE_ROW_BITS, body, jnp.zeros((1, qb), I32))

        last = tie_scr[0:1, :]

        def mark(r0, rows, carry):
            row = r0 + lax.broadcasted_iota(I32, (rows, qb), 0)
            col = t0 + lax.broadcasted_iota(I32, (rows, qb), 1)
            x = sel_scr[pl.ds(r0, rows), :]
            dist = (row - col).astype(F32)
            sel_scr[pl.ds(r0, rows), :] = jnp.where(
                x == thr, jnp.where(row <= last, dist, NEG_BIG), jnp.where(x > thr, dist, NEG_BIG))
            return carry

        _for_key_rows(n_chunks, mark, 0)

    qh = [qt_ref[u * HEAD_DIM:(u + 1) * HEAD_DIM, :] for u in range(hps)]
    slope = [slope_ref[u, 0:1, :] for u in range(hps)]

    def logits(r0, rows, mx):
        kk = k_ref[pl.ds(r0, rows), :]
        nd = sel_scr[pl.ds(r0, rows), :]
        out = []
        for u in range(hps):
            s = jnp.dot(kk[:, u * HEAD_DIM:(u + 1) * HEAD_DIM], qh[u], preferred_element_type=F32)
            s = s + nd * slope[u]
            s_scr[u, pl.ds(r0, rows), :] = s
            out.append(jnp.maximum(mx[u], s.reshape(rows // 8, 8, qb).max(axis=0)))
        return tuple(out)

    mx8 = _for_key_rows(n_chunks, logits, tuple(jnp.full((8, qb), -3e38, F32) for _ in range(hps)))
    m = [x.max(axis=0, keepdims=True) for x in mx8]

    def weigh(r0, rows, carry):
        c0 = r0 // sc
        out = []
        for u in range(hps):
            l8, acc = carry[u]
            p = jnp.exp2(s_scr[u, pl.ds(r0, rows), :] - m[u])
            l8 = l8 + p.reshape(rows // 8, 8, qb).sum(axis=0)
            v = jnp.concatenate([vt_ref[u, c0 + j] for j in range(rows // sc)], axis=1)
            acc = acc + jnp.dot(v, p.astype(BF16), preferred_element_type=F32)
            out.append((l8, acc))
        return tuple(out)

    zero = (jnp.zeros((8, qb), F32), jnp.zeros((HEAD_DIM, qb), F32))
    res = _for_key_rows(n_chunks, weigh, tuple(zero for _ in range(hps)))
    for u in range(hps):
        l8, acc = res[u]
        o_ref[u * HEAD_DIM:(u + 1) * HEAD_DIM, :] = acc / l8.sum(axis=0, keepdims=True)


def _prompt_attention(qit, wt, ki, qt, kv_bf, vt4, slopes, t_pad):
    n_blk = t_pad // Q_BLOCK
    hps = HEADS_PER_STEP
    assert t_pad <= TIE_ROW_LIMIT and t_pad % KEY_CHUNKS[-1] == 0
    return pl.pallas_call(
        _prompt_attn_kernel,
        grid=(n_blk, N_HEADS // hps),
        in_specs=[pl.BlockSpec((N_IDX_HEADS * IDX_DIM, Q_BLOCK), lambda i, h: (0, i)),
                  pl.BlockSpec((N_IDX_HEADS, Q_BLOCK), lambda i, h: (0, i)),
                  pl.BlockSpec((t_pad, IDX_DIM), lambda i, h: (0, 0)),
                  pl.BlockSpec((hps * HEAD_DIM, Q_BLOCK), lambda i, h: (h, i)),
                  pl.BlockSpec((t_pad, hps * HEAD_DIM), lambda i, h: (0, h)),
                  pl.BlockSpec((hps, t_pad // S_CHUNK, HEAD_DIM, S_CHUNK), lambda i, h: (h, 0, 0, 0)),
                  pl.BlockSpec((hps, 8, Q_BLOCK), lambda i, h: (h, 0, 0))],
        out_specs=pl.BlockSpec((hps * HEAD_DIM, Q_BLOCK), lambda i, h: (h, i)),
        out_shape=jax.ShapeDtypeStruct((N_HEADS * HEAD_DIM, t_pad), F32),
        scratch_shapes=[pltpu.VMEM((t_pad, Q_BLOCK), F32), pltpu.VMEM((hps, t_pad, Q_BLOCK), F32),
                        pltpu.VMEM((8, Q_BLOCK), I32)],
        compiler_params=_cparams("arbitrary", "arbitrary"),
        name="prompt_attention",
    )(qit, wt, ki, qt, kv_bf, vt4, slopes)


N_SROWS = 4 * N_HEADS
PAGE_ROWS = PAGE * N_HEADS
PAGES_PER_STEP = 8
PAGES_PER_GROUP = 4


def _sample_score_kernel(pt_ref, *refs, n_pages):
    kidx_refs = refs[:n_pages]
    kinew_ref, qi_ref, w_ref, o_ref = refs[n_pages:]
    qi = qi_ref[...]
    w = w_ref[...]

    def page_scores(page):
        s = jnp.dot(qi, page.astype(BF16), preferred_element_type=F32)
        x = jnp.maximum(s, 0.0) * w
        x8 = x.reshape(N_IDX_HEADS // 2, 8, PAGE).sum(axis=0)
        return x8 + pltpu.roll(x8, 4, 0)

    for p in range(n_pages):
        o_ref[:, p * PAGE:(p + 1) * PAGE] = page_scores(kidx_refs[p][...])
    slot = lax.broadcasted_iota(I32, (8, PAGE), 1)
    q = lax.broadcasted_iota(I32, (8, PAGE), 0) % 4
    o_ref[:, n_pages * PAGE:] = jnp.where(slot <= q, page_scores(kinew_ref[...]), NEG_BIG)


def _sample_scores(page_table, cache_kidx_t, kinew_t, qi_s, w_s):
    b, n_pages = page_table.shape

    def page_index(s, pt, *, p):
        return (0, pt[s, p], 0, 0)

    grid_spec = pltpu.PrefetchScalarGridSpec(
        num_scalar_prefetch=1,
        grid=(b,),
        in_specs=[pl.BlockSpec((None, None, IDX_DIM, PAGE), functools.partial(page_index, p=p))
                  for p in range(n_pages)]
        + [pl.BlockSpec((None, IDX_DIM, PAGE), lambda s, pt: (s, 0, 0)),
           pl.BlockSpec((None, N_SROWS, IDX_DIM), lambda s, pt: (s, 0, 0)),
           pl.BlockSpec((None, N_SROWS, PAGE), lambda s, pt: (s, 0, 0))],
        out_specs=pl.BlockSpec((None, 8, (n_pages + 1) * PAGE), lambda s, pt: (s, 0, 0)),
    )
    return pl.pallas_call(
        functools.partial(_sample_score_kernel, n_pages=n_pages),
        grid_spec=grid_spec,
        out_shape=jax.ShapeDtypeStruct((b, 8, (n_pages + 1) * PAGE), F32),
        compiler_params=_cparams("arbitrary"),
        name="sample_scores",
    )(page_table, *([cache_kidx_t] * n_pages), kinew_t, qi_s, w_s)


def _sample_select_kernel(s_ref, e_ref, o_ref, tie_scr):
    sc = s_ref[...]
    rows, width = sc.shape

    def count_ge(cand):
        return jnp.sum(jnp.where(sc >= cand, 1.0, 0.0), axis=1, keepdims=True)

    thr, n_ge = _bisect_threshold(count_ge, jnp.zeros((rows, 1), I32))
    pos = lax.broadcasted_iota(I32, (rows, width), 1)
    tie_scr[...] = jnp.full(tie_scr.shape, TIE_ROW_LIMIT, I32)

    @pl.when(jnp.max(n_ge) > TOPK)
    def _():
        need = TOPK - jnp.sum(jnp.where(sc > thr, 1.0, 0.0), axis=1, keepdims=True)

        def tie_body(p, last):
            cand = last + jnp.left_shift(jnp.int32(1), TIE_ROW_BITS - 1 - p)
            below = jnp.sum(jnp.where(sc == thr, jnp.where(pos < cand, 1.0, 0.0), 0.0), axis=1, keepdims=True)
            return jnp.where(below < need, cand, last)

        tie_scr[...] = lax.fori_loop(0, TIE_ROW_BITS, tie_body, jnp.zeros((rows, 1), I32))

    last = tie_scr[...]
    sel = jnp.where(sc == thr, jnp.where(pos <= last, 1.0, 0.0), jnp.where(sc > thr, 1.0, 0.0)).astype(BF16)
    n_tiles = width // PAGE
    for t in range(n_tiles):
        x = jnp.dot(sel[:, t * PAGE:(t + 1) * PAGE], e_ref[...], preferred_element_type=F32)
        lo = t * PAGE_ROWS
        hi = min(lo + PAGE_ROWS, o_ref.shape[1])
        o_ref[:, lo:hi] = x[:, :hi - lo]


def _sample_select(scores, expand, out_width):
    rows, width = scores.shape
    rb = 128
    return pl.pallas_call(
        _sample_select_kernel,
        grid=(rows // rb,),
        in_specs=[pl.BlockSpec((rb, width), lambda i: (i, 0)),
                  pl.BlockSpec((PAGE, PAGE_ROWS), lambda i: (0, 0))],
        out_specs=pl.BlockSpec((rb, out_width), lambda i: (i, 0)),
        out_shape=jax.ShapeDtypeStruct((rows, out_width), F32),
        scratch_shapes=[pltpu.VMEM((rb, 1), I32)],
        compiler_params=_cparams("parallel"),
        name="sample_select",
    )(scores, expand)


def _sample_attn_kernel(pt_ref, q_ref, *refs, n_steps, past_len):
    pps = PAGES_PER_STEP
    k_refs, v_refs = refs[:pps], refs[pps:2 * pps]
    kn_ref, vn_ref, selp_ref, seln_ref, slope_ref, hm_ref, o_ref, m_scr, l_scr, acc_scr = refs[2 * pps:]
    jj = pl.program_id(1)

    @pl.when(jj == 0)
    def _():
        m_scr[...] = jnp.full(m_scr.shape, -3e38, F32)
        l_scr[...] = jnp.zeros(l_scr.shape, F32)
        acc_scr[...] = jnp.zeros(acc_scr.shape, F32)

    def step(k_list, v_list, sel8, pos0):
        q = q_ref[...]
        s = jnp.concatenate(
            [lax.dot_general(q, k.astype(BF16), (((1,), (1,)), ((), ())), preferred_element_type=F32)
             for k in k_list], axis=1)
        width = s.shape[1]
        lane = lax.broadcasted_iota(I32, (8, width), 1)
        rq = lax.broadcasted_iota(I32, (8, width), 0) % 4
        dist = (pos0 - past_len + lane // N_HEADS - rq).astype(F32)
        hm = jnp.concatenate([hm_ref[...]] * (width // PAGE_ROWS), axis=1) if width > PAGE_ROWS \
            else hm_ref[:, :width]
        g = N_SROWS // 8
        x = s.reshape(g, 8, width) + slope_ref[...].reshape(g, 8, 1) * dist[None]
        x = jnp.where(sel8[None] > 0.5, x, NEG_BIG) + hm.reshape(g, 8, width)
        x = x.reshape(N_SROWS, width)
        m = m_scr[...]
        m_new = jnp.maximum(m, jnp.max(x, axis=1, keepdims=True))
        alpha = jnp.exp2(m - m_new)
        p = jnp.exp2(x - m_new)
        l_scr[...] = alpha * l_scr[...] + jnp.sum(p, axis=1, keepdims=True)
        pb = p.astype(BF16)
        pv = None
        off = 0
        for v in v_list:
            part = jnp.dot(pb[:, off:off + v.shape[0]], v.astype(BF16), preferred_element_type=F32)
            pv = part if pv is None else pv + part
            off += v.shape[0]
        acc_scr[...] = alpha * acc_scr[...] + pv
        m_scr[...] = m_new

    ppg = PAGES_PER_GROUP
    for g in range(pps // ppg):
        lanes = slice(g * ppg * PAGE_ROWS, (g + 1) * ppg * PAGE_ROWS)
        step([r[...] for r in k_refs[g * ppg:(g + 1) * ppg]], [r[...] for r in v_refs[g * ppg:(g + 1) * ppg]],
             selp_ref[:, lanes], (jj * pps + g * ppg) * PAGE)

    @pl.when(jj == n_steps - 1)
    def _():
        step([kn_ref[...]], [vn_ref[...]], seln_ref[...], past_len)
        o_ref[...] = acc_scr[...] / l_scr[...]


def _sample_attention(page_table, q_s, cache_k2, cache_v2, knew, vnew, sel_exp, slope_rows, head_mask):
    b, n_pages = page_table.shape
    pps = PAGES_PER_STEP
    assert n_pages % pps == 0 and pps % PAGES_PER_GROUP == 0
    n_steps = n_pages // pps
    nw = knew.shape[1]

    def page_index(s, jj, pt, *, u):
        return (pt[s, jj * pps + u], 0)

    page_specs = [pl.BlockSpec((PAGE_ROWS, HEAD_DIM), functools.partial(page_index, u=u)) for u in range(pps)]
    grid_spec = pltpu.PrefetchScalarGridSpec(
        num_scalar_prefetch=1,
        grid=(b, n_steps),
        in_specs=[pl.BlockSpec((None, N_SROWS, HEAD_DIM), lambda s, jj, pt: (s, 0, 0))]
        + page_specs + page_specs
        + [pl.BlockSpec((None, nw, HEAD_DIM), lambda s, jj, pt: (s, 0, 0)),
           pl.BlockSpec((None, nw, HEAD_DIM), lambda s, jj, pt: (s, 0, 0)),
           pl.BlockSpec((None, 8, pps * PAGE_ROWS), lambda s, jj, pt: (s, 0, jj)),
           pl.BlockSpec((None, 8, nw), lambda s, jj, pt: (s, 0, n_pages * PAGE_ROWS // nw)),
           pl.BlockSpec((N_SROWS, 1), lambda s, jj, pt: (0, 0)),
           pl.BlockSpec((N_SROWS, PAGE_ROWS), lambda s, jj, pt: (0, 0))],
        out_specs=pl.BlockSpec((None, N_SROWS, HEAD_DIM), lambda s, jj, pt: (s, 0, 0)),
        scratch_shapes=[pltpu.VMEM((N_SROWS, 1), F32), pltpu.VMEM((N_SROWS, 1), F32),
                        pltpu.VMEM((N_SROWS, HEAD_DIM), F32)],
    )
    return pl.pallas_call(
        functools.partial(_sample_attn_kernel, n_steps=n_steps, past_len=n_pages * PAGE),
        grid_spec=grid_spec,
        out_shape=jax.ShapeDtypeStruct((b, N_SROWS, HEAD_DIM), F32),
        compiler_params=_cparams("arbitrary", "arbitrary"),
        name="sample_attention",
    )(page_table, q_s, *([cache_k2] * pps), *([cache_v2] * pps), knew, vnew, sel_exp, sel_exp, slope_rows,
      head_mask)


def _merge_out_kernel(ot_ref, os_ref, ga_ref, y_ref, wa_ref, wb_ref, wo_ref, g1_ref, g2_ref, x_ref, fg_ref,
                      out_ref, *, n_prompt_blocks):
    o = jnp.where(pl.program_id(0) < n_prompt_blocks, ot_ref[...].T, os_ref[...])
    ga = ga_ref[...]
    og = (o * (ga * _sigmoid(ga))).astype(BF16)
    pa = jnp.dot(y_ref[...], wa_ref[...], preferred_element_type=F32)
    pb = jnp.dot(og, wb_ref[...], preferred_element_type=F32)
    merged = (_sigmoid(g1_ref[...]) * pa + _sigmoid(g2_ref[...]) * pb).astype(BF16)
    res = x_ref[...] + jnp.dot(merged, wo_ref[...], preferred_element_type=F32)
    y = res * lax.rsqrt(jnp.mean(res * res, axis=-1, keepdims=True) + NORM_EPS)
    out_ref[...] = y * fg_ref[...]


def _merge_out(ot, o_s, ga, y_rnn, w_pa, w_pb, w_out, gg, x, fg):
    m, d = ga.shape
    npb = ot.shape[1] // ROW_BLOCK
    row = lambda i: (i, 0)
    weight = pl.BlockSpec((d, d), lambda i: (0, 0), pipeline_mode=pl.Buffered(1))
    return pl.pallas_call(
        functools.partial(_merge_out_kernel, n_prompt_blocks=npb),
        grid=(m // ROW_BLOCK,),
        in_specs=[pl.BlockSpec((d, ROW_BLOCK), lambda i: (0, jnp.minimum(i, npb - 1))),
                  pl.BlockSpec((ROW_BLOCK, d), lambda i: (jnp.maximum(i - npb, 0), 0)),
                  pl.BlockSpec((ROW_BLOCK, d), row), pl.BlockSpec((ROW_BLOCK, d), row),
                  weight, weight, weight,
                  pl.BlockSpec((ROW_BLOCK, d), row), pl.BlockSpec((ROW_BLOCK, d), lambda i: (i, 1)),
                  pl.BlockSpec((ROW_BLOCK, d), row), pl.BlockSpec((1, d), lambda i: (0, 0))],
        out_specs=pl.BlockSpec((ROW_BLOCK, d), row),
        out_shape=jax.ShapeDtypeStruct((m, d), F32),
        compiler_params=pltpu.CompilerParams(dimension_semantics=("parallel",),
                                             vmem_limit_bytes=MERGE_VMEM_LIMIT),
        name="merge_out",
    )(ot, o_s, ga, y_rnn, w_pa, w_pb, w_out, gg, gg, x, fg.reshape(1, d))


def kernel(x_prompt, x_sample, cache_k, cache_v, cache_kidx, state_h, state_conv, page_table, meta_tokens,
           norm_g, w_in, conv_w, conv_b, lru_wa, lru_ba, lru_wx, lru_bx, lru_lambda, w_proj_a, w_proj_b,
           w_out, final_g):
    assert x_prompt.shape[0] == 1 and norm_g.shape[0] == 1
    d = D_MODEL
    seq = x_prompt.shape[1]
    t_p = seq + N_META
    t_pad = -(-t_p // Q_BLOCK) * Q_BLOCK
    n_seq, n_new = x_sample.shape[:2]
    n_s = n_seq * n_new
    n_pages = page_table.shape[1]
    past_len = n_pages * PAGE
    assert n_new == 4 and n_s % ROW_BLOCK == 0
    m_all = t_pad + n_s

    x_all = jnp.concatenate([meta_tokens.astype(F32), x_prompt[0], jnp.zeros((t_pad - t_p, d), F32),
                             x_sample.reshape(n_s, d)], axis=0)
    xn = _rmsnorm_bf16(x_all, norm_g[0])

    wb = w_in[0].astype(BF16)
    cuts = np.cumsum([d, d, d, d, d, d, N_IDX_HEADS * IDX_DIM, IDX_DIM, N_IDX_HEADS, d, d]).tolist()
    rg = _matmul(xn, wb, F32, cols=(0, cuts[1]), name="proj_rnn")
    qv = _matmul(xn, wb, BF16, cols=(cuts[1], cuts[2]), scale=LOG2E * HEAD_DIM ** -0.5, name="proj_q")
    kv, kv_bf = _matmul(xn, wb, (F32, BF16), cols=(cuts[2], cuts[4]), name="proj_kv")
    ga = _matmul(xn, wb, F32, cols=(cuts[4], cuts[5]), name="proj_ga")
    qi = _matmul(xn, wb, BF16, cols=(cuts[5], cuts[6]), scale=IDX_DIM ** -0.5, name="proj_qi")
    w_kw = jnp.pad(wb[:, cuts[6]:cuts[8]], ((0, 0), (0, 128 - (cuts[8] - cuts[6]))))
    kw = _matmul(xn, w_kw, F32, name="proj_kiwi")
    gg = _matmul(xn, wb[:, cuts[8]:], F32, name="proj_gates")

    xr, gr = rg[:, :d], rg[:, d:]
    k_all, v_all = kv[:, :d], kv[:, d:]
    ki_all = kw[:, :IDX_DIM]
    wi_all = kw[:, IDX_DIM:IDX_DIM + N_IDX_HEADS] * (N_IDX_HEADS ** -0.5)

    cw, cb = conv_w[0], conv_b[0].reshape(1, d)
    wa, wx = lru_wa[0].astype(BF16), lru_wx[0].astype(BF16)
    ba, bx, lam = lru_ba[0].reshape(1, d), lru_bx[0].reshape(1, d), lru_lambda[0].reshape(1, d)
    hs_p, y_p = _lru_prompt(rg, cw, cb, wa, wx, ba, bx, lam, t_pad)
    xr_s = jnp.swapaxes(xr[t_pad:].reshape(n_seq, n_new, d), 0, 1)
    gr_s = jnp.swapaxes(gr[t_pad:].reshape(n_seq, n_new, d), 0, 1)
    xs = jnp.concatenate([jnp.swapaxes(state_conv[0], 0, 1), xr_s], axis=0)
    hs_s, y_s = _lru_sample(xs, gr_s, state_h[0], cw, cb, wa, wx, ba, bx, lam)
    y_rnn = jnp.concatenate([y_p, jnp.swapaxes(y_s, 0, 1).reshape(n_s, d)], axis=0)

    slopes = jnp.asarray(LOG2E * 2.0 ** (-8.0 * np.arange(1, N_HEADS + 1) / N_HEADS), dtype=F32)
    qit = qi.T
    wt = wi_all.T
    qt = qv.T
    vt4 = kv_bf[:t_pad, d:].reshape(t_pad // S_CHUNK, S_CHUNK, N_HEADS, HEAD_DIM).transpose(2, 0, 3, 1)
    slope_b = jnp.broadcast_to(slopes[:, None, None], (N_HEADS, 8, Q_BLOCK))
    ot = _prompt_attention(qit, wt, ki_all.astype(BF16), qt, kv_bf, vt4, slope_b, t_pad)

    hq = lambda a, width: a.reshape(n_seq, n_new, N_HEADS, width).transpose(0, 2, 1, 3).reshape(
        n_seq, N_HEADS * n_new, width)
    qi_s = hq(qi[t_pad:], IDX_DIM)
    w_s = jnp.broadcast_to(
        wi_all[t_pad:].reshape(n_seq, n_new, N_IDX_HEADS).transpose(0, 2, 1).reshape(n_seq, -1, 1),
        (n_seq, N_IDX_HEADS * n_new, PAGE))
    kinew_t = jnp.pad(jnp.swapaxes(ki_all[t_pad:].reshape(n_seq, n_new, IDX_DIM), 1, 2),
                      ((0, 0), (0, 0), (0, PAGE - n_new)))
    scores = _sample_scores(page_table, jnp.swapaxes(cache_kidx, 2, 3), kinew_t, qi_s, w_s)
    col = np.arange(PAGE_ROWS)
    expand = jnp.asarray(col[None, :] // N_HEADS == np.arange(PAGE)[:, None], dtype=BF16)
    head_mask = jnp.asarray(np.where(np.arange(N_SROWS)[:, None] // n_new == col[None, :] % N_HEADS,
                                     0.0, 4.0 * NEG_BIG), dtype=F32)
    sel_exp = _sample_select(scores.reshape(n_seq * 8, past_len + PAGE), expand, (past_len + 8) * N_HEADS)
    sel_exp = sel_exp.reshape(n_seq, 8, (past_len + 8) * N_HEADS)
    new_rows = lambda a: jnp.pad(a.reshape(n_seq, n_new * N_HEADS, HEAD_DIM),
                                 ((0, 0), (0, (8 - n_new) * N_HEADS), (0, 0)))
    slope_rows = jnp.repeat(slopes, n_new).reshape(N_SROWS, 1)
    o_s = _sample_attention(page_table, hq(qv[t_pad:], HEAD_DIM),
                            cache_k.reshape(-1, HEAD_DIM), cache_v.reshape(-1, HEAD_DIM),
                            new_rows(k_all[t_pad:]), new_rows(v_all[t_pad:]), sel_exp, slope_rows, head_mask)
    o_s = o_s.reshape(n_seq, N_HEADS, n_new, HEAD_DIM).transpose(0, 2, 1, 3).reshape(n_s, d)

    y_all = _merge_out(ot, o_s, ga, y_rnn, w_proj_a[0].astype(BF16), w_proj_b[0].astype(BF16),
                       w_out[0].astype(BF16), gg, x_all, final_g)

    y_prompt = y_all[N_META:t_p][None]
    y_sample = y_all[t_pad:].reshape(n_seq, n_new, d)
    heads = lambda a, *lead: a.reshape(*lead, N_HEADS, HEAD_DIM)
    k_prompt = heads(k_all[:t_p], 1, 1, t_p)
    v_prompt = heads(v_all[:t_p], 1, 1, t_p)
    kidx_prompt = ki_all[:t_p][None, None]
    h_prompt = hs_p[t_p - 1][None, None]
    conv_prompt = xr[t_p - (CONV_W - 1):t_p][None, None]
    k_sample = heads(k_all[t_pad:], 1, n_seq, n_new)
    v_sample = heads(v_all[t_pad:], 1, n_seq, n_new)
    kidx_sample = ki_all[t_pad:].reshape(1, n_seq, n_new, IDX_DIM)
    h_sample = hs_s[n_new - 1][None]
    conv_sample = jnp.swapaxes(xs[n_new:], 0, 1)[None]
    return (y_prompt, y_sample, k_prompt, v_prompt, kidx_prompt, h_prompt, conv_prompt,
            k_sample, v_sample, kidx_sample, h_sample, conv_sample)
```

```python
import functools

import numpy as np
import jax
import jax.numpy as jnp
from jax import lax
from jax.experimental import pallas as pl
from jax.experimental.pallas import tpu as pltpu

F32 = jnp.float32
BF16 = jnp.bfloat16
I32 = jnp.int32

D_MODEL = 2048
N_HEADS = 16
HEAD_DIM = 128
N_IDX_HEADS = 16
IDX_DIM = 64
N_LRU_BLOCKS = 16
LRU_BLOCK = 128
CONV_W = 4
LRU_C = 8.0
N_META = 16
TOPK = 256
PAGE = 128
NORM_EPS = 1e-6
NEG_BIG = -1e30
INT_MIN = -(2 ** 31)

ROW_BLOCK = 256
Q_BLOCK = 256
S_CHUNK = 256
KEY_CHUNKS = (4096, 2048, 1024, S_CHUNK)
HEADS_PER_STEP = 2
LOG2E = 1.4426950408889634
VMEM_LIMIT = 56 * 1024 * 1024
MERGE_VMEM_LIMIT = 60 * 1024 * 1024


def _cparams(*sem):
    return pltpu.CompilerParams(dimension_semantics=sem, vmem_limit_bytes=VMEM_LIMIT)


def _rmsnorm_kernel(x_ref, g_ref, o_ref):
    x = x_ref[...]
    y = x * lax.rsqrt(jnp.mean(x * x, axis=-1, keepdims=True) + NORM_EPS)
    o_ref[...] = (y * g_ref[...]).astype(o_ref.dtype)


def _rmsnorm_bf16(x, g):
    m, d = x.shape
    return pl.pallas_call(
        _rmsnorm_kernel,
        grid=(m // ROW_BLOCK,),
        in_specs=[pl.BlockSpec((ROW_BLOCK, d), lambda i: (i, 0)),
                  pl.BlockSpec((1, d), lambda i: (0, 0))],
        out_specs=pl.BlockSpec((ROW_BLOCK, d), lambda i: (i, 0)),
        out_shape=jax.ShapeDtypeStruct((m, d), BF16),
        compiler_params=_cparams("parallel"),
        name="rmsnorm_in",
    )(x, g.reshape(1, d))


def _mm_kernel(x_ref, w_ref, *o_refs, scale):
    acc = jnp.dot(x_ref[...], w_ref[...], preferred_element_type=F32)
    if scale is not None:
        acc = acc * scale
    for o_ref in o_refs:
        o_ref[...] = acc.astype(o_ref.dtype)


def _matmul(x, w, out_dtypes, *, cols=None, scale=None, tm=1280, tn=1024, name="matmul"):
    m, k = x.shape
    lo, hi = (0, w.shape[1]) if cols is None else cols
    n = hi - lo
    tn = min(tn, n)
    assert m % tm == 0 and n % tn == 0 and lo % tn == 0
    j0 = lo // tn
    single = not isinstance(out_dtypes, tuple)
    dts = (out_dtypes,) if single else out_dtypes
    outs = pl.pallas_call(
        functools.partial(_mm_kernel, scale=scale),
        grid=(n // tn, m // tm),
        in_specs=[pl.BlockSpec((tm, k), lambda j, i: (i, 0)),
                  pl.BlockSpec((k, tn), lambda j, i: (0, j0 + j))],
        out_specs=[pl.BlockSpec((tm, tn), lambda j, i: (i, j)) for _ in dts],
        out_shape=[jax.ShapeDtypeStruct((m, n), dt) for dt in dts],
        compiler_params=_cparams("parallel", "parallel"),
        name=name,
    )(x, w)
    return outs[0] if single else outs


def _softplus(x):
    return jnp.maximum(x, 0.0) + jnp.log1p(jnp.exp(-jnp.abs(x)))


def _sigmoid(x):
    return 0.5 * jnp.tanh(0.5 * x) + 0.5


def _lru_coeffs(xc, wa_ref, wx_ref, ba, bx, lam):
    rs, is_ = [], []
    for n in range(N_LRU_BLOCKS):
        sl = slice(n * LRU_BLOCK, (n + 1) * LRU_BLOCK)
        xb = xc[:, sl].astype(BF16)
        rs.append(jnp.dot(xb, wa_ref[n], preferred_element_type=F32))
        is_.append(jnp.dot(xb, wx_ref[n], preferred_element_type=F32))
    r = _sigmoid(jnp.concatenate(rs, axis=-1) + ba)
    i = _sigmoid(jnp.concatenate(is_, axis=-1) + bx)
    log_a = (-LRU_C * _softplus(-lam)) * r
    a = jnp.exp(log_a)
    u = jnp.sqrt(1.0 - a * a) * (i * xc)
    return a, u


def _lru_prompt_kernel(xr_ref, gr_ref, cw_ref, cb_ref, wa_ref, wx_ref, ba_ref, bx_ref, lam_ref,
                       hs_ref, y_ref, xp_scr, a_scr, u_scr, h_scr):
    tb = ROW_BLOCK

    @pl.when(pl.program_id(0) == 0)
    def _():
        xp_scr[0:8, :] = jnp.zeros((8, D_MODEL), F32)
        h_scr[...] = jnp.zeros((8, D_MODEL), F32)

    x = xr_ref[...]
    xp_scr[8:8 + tb, :] = x
    xc = cb_ref[...] + xp_scr[5:5 + tb, :] * cw_ref[0:1, :]
    xc = xc + xp_scr[6:6 + tb, :] * cw_ref[1:2, :]
    xc = xc + xp_scr[7:7 + tb, :] * cw_ref[2:3, :]
    xc = xc + x * cw_ref[3:4, :]
    xp_scr[0:8, :] = x[tb - 8:tb, :]

    a, u = _lru_coeffs(xc, wa_ref, wx_ref, ba_ref[...], bx_ref[...], lam_ref[...])
    a_scr[...] = a
    u_scr[...] = u

    def group(g, h):
        r0 = pl.multiple_of(g * 8, 8)
        a8 = a_scr[pl.ds(r0, 8), :]
        u8 = u_scr[pl.ds(r0, 8), :]
        rows = []
        for r in range(8):
            h = a8[r:r + 1, :] * h + u8[r:r + 1, :]
            rows.append(h)
        hs_ref[pl.ds(r0, 8), :] = jnp.concatenate(rows, axis=0)
        return h

    h_last = lax.fori_loop(0, tb // 8, group, h_scr[0:1, :])
    h_scr[0:1, :] = h_last

    g = gr_ref[...]
    y_ref[...] = (hs_ref[...] * (g * _sigmoid(g))).astype(y_ref.dtype)


def _lru_prompt(rg, cw, cb, wa, wx, ba, bx, lam, n_rows):
    d = D_MODEL
    row = lambda i: (i, 0)
    const2 = lambda i: (0, 0)
    const3 = lambda i: (0, 0, 0)
    return pl.pallas_call(
        _lru_prompt_kernel,
        grid=(n_rows // ROW_BLOCK,),
        in_specs=[pl.BlockSpec((ROW_BLOCK, d), row), pl.BlockSpec((ROW_BLOCK, d), lambda i: (i, 1)),
                  pl.BlockSpec((CONV_W, d), const2), pl.BlockSpec((1, d), const2),
                  pl.BlockSpec((N_LRU_BLOCKS, LRU_BLOCK, LRU_BLOCK), const3),
                  pl.BlockSpec((N_LRU_BLOCKS, LRU_BLOCK, LRU_BLOCK), const3),
                  pl.BlockSpec((1, d), const2), pl.BlockSpec((1, d), const2), pl.BlockSpec((1, d), const2)],
        out_specs=[pl.BlockSpec((ROW_BLOCK, d), row), pl.BlockSpec((ROW_BLOCK, d), row)],
        out_shape=[jax.ShapeDtypeStruct((n_rows, d), F32), jax.ShapeDtypeStruct((n_rows, d), BF16)],
        scratch_shapes=[pltpu.VMEM((8 + ROW_BLOCK, d), F32), pltpu.VMEM((ROW_BLOCK, d), F32),
                        pltpu.VMEM((ROW_BLOCK, d), F32), pltpu.VMEM((8, d), F32)],
        compiler_params=_cparams("arbitrary"),
        name="lru_prompt",
    )(rg, rg, cw, cb, wa, wx, ba, bx, lam)


def _lru_sample_kernel(xs_ref, gr_ref, h0_ref, cw_ref, cb_ref, wa_ref, wx_ref, ba_ref, bx_ref, lam_ref,
                       hs_ref, y_ref):
    n_t = hs_ref.shape[0]
    h = h0_ref[...]
    for t in range(n_t):
        xc = cb_ref[...] + xs_ref[t] * cw_ref[0:1, :]
        for j in range(1, CONV_W):
            xc = xc + xs_ref[t + j] * cw_ref[j:j + 1, :]
        a, u = _lru_coeffs(xc, wa_ref, wx_ref, ba_ref[...], bx_ref[...], lam_ref[...])
        h = a * h + u
        hs_ref[t] = h
        g = gr_ref[t]
        y_ref[t] = (h * (g * _sigmoid(g))).astype(y_ref.dtype)


def _lru_sample(xs, gr, h0, cw, cb, wa, wx, ba, bx, lam):
    n_t, b, d = gr.shape
    return pl.pallas_call(
        _lru_sample_kernel,
        out_shape=[jax.ShapeDtypeStruct((n_t, b, d), F32), jax.ShapeDtypeStruct((n_t, b, d), BF16)],
        compiler_params=pltpu.CompilerParams(vmem_limit_bytes=VMEM_LIMIT),
        name="lru_sample",
    )(xs, gr, h0, cw, cb, wa, wx, ba, bx, lam)


LOWEST_KEY = INT_MIN + 0x00800000
TIE_ROW_BITS = 14
TIE_ROW_LIMIT = 2 ** TIE_ROW_BITS - 1


def _key_to_f32(key):
    return pltpu.bitcast(key ^ ((key >> 31) & 0x7FFFFFFF), F32)


def _bisect_threshold(count_ge, zero):
    c0 = count_ge(jnp.zeros(zero.shape, F32))
    thr = jnp.where(c0 >= TOPK, zero, zero + INT_MIN)
    cnt = jnp.where(c0 >= TOPK, c0, 0.0)

    def body(p, carry):
        thr, cnt = carry
        cand = thr + jnp.left_shift(jnp.int32(1), 30 - p)
        c = count_ge(_key_to_f32(cand))
        return jnp.where(c >= TOPK, cand, thr), jnp.where(c >= TOPK, c, cnt)

    thr, cnt = lax.fori_loop(0, 31, body, (thr, cnt))
    return _key_to_f32(jnp.maximum(thr, LOWEST_KEY)), cnt


def _for_key_rows(n_small, body, carry):
    done = 0
    for size in KEY_CHUNKS:
        n_this = (n_small * S_CHUNK - done) // size
        carry = lax.fori_loop(
            0, n_this, lambda c, x, d=done, sz=size: body(pl.multiple_of(d + c * sz, sz), sz, x), carry)
        done = done + n_this * size
    return carry


def _prompt_attn_kernel(qit_ref, wt_ref, ki_ref, qt_ref, k_ref, vt_ref, slope_ref, o_ref, sel_scr, s_scr,
                        tie_scr):
    i = pl.program_id(0)
    h = pl.program_id(1)
    hps = HEADS_PER_STEP
    n_chunks = i + 1
    t0 = i * Q_BLOCK
    sc, qb = S_CHUNK, Q_BLOCK

    @pl.when(h == 0)
    def _select():
        def score_chunk(c, carry):
            r0 = pl.multiple_of(c * sc, sc)
            kc = ki_ref[pl.ds(r0, sc), :]
            acc = jnp.zeros((sc, qb), F32)
            for hh in range(N_IDX_HEADS):
                s = jnp.dot(kc, qit_ref[hh * IDX_DIM:(hh + 1) * IDX_DIM, :], preferred_element_type=F32)
                acc = acc + jnp.maximum(s, 0.0) * wt_ref[hh:hh + 1, :]
            row = r0 + lax.broadcasted_iota(I32, (sc, qb), 0)
            col = t0 + lax.broadcasted_iota(I32, (sc, qb), 1)
            sel_scr[pl.ds(r0, sc), :] = jnp.where(row <= col, acc, -jnp.inf)
            return carry

        lax.fori_loop(0, n_chunks, score_chunk, 0)

        def count_ge(cand):
            def body(r0, rows, cnt):
                ind = jnp.where(sel_scr[pl.ds(r0, rows), :] >= cand, 1.0, 0.0)
                return cnt + ind.reshape(rows // 8, 8, qb).sum(axis=0)

            cnt8 = _for_key_rows(n_chunks, body, jnp.zeros((8, qb), F32))
            return cnt8.sum(axis=0, keepdims=True)

        thr, n_ge = _bisect_threshold(count_ge, jnp.zeros((1, qb), I32))

        def count_rows(indicator):
            def body(r0, rows, cnt):
                row = r0 + lax.broadcasted_iota(I32, (rows, qb), 0)
                ind = indicator(sel_scr[pl.ds(r0, rows), :], row)
                return cnt + ind.reshape(rows // 8, 8, qb).sum(axis=0)

            return _for_key_rows(n_chunks, body, jnp.zeros((8, qb), F32)).sum(axis=0, keepdims=True)

        tie_scr[...] = jnp.full(tie_scr.shape, TIE_ROW_LIMIT, I32)

        @pl.when(jnp.max(n_ge) > TOPK)
        def _():
            need = TOPK - count_rows(lambda x, row: jnp.where(x > thr, 1.0, 0.0))

            def body(p, last):
                cand = last + jnp.left_shift(jnp.int32(1), TIE_ROW_BITS - 1 - p)
                below = count_rows(lambda x, row: jnp.where(x == thr, jnp.where(row < cand, 1.0, 0.0), 0.0))
                return jnp.where(below < need, cand, last)

            tie_scr[0:1, :] = lax.fori_loop(0, TIE_ROW_BITS, body, jnp.zeros((1, qb), I32))

        last = tie_scr[0:1, :]

        def mark(r0, rows, carry):
            row = r0 + lax.broadcasted_iota(I32, (rows, qb), 0)
            col = t0 + lax.broadcasted_iota(I32, (rows, qb), 1)
            x = sel_scr[pl.ds(r0, rows), :]
            dist = (row - col).astype(F32)
            sel_scr[pl.ds(r0, rows), :] = jnp.where(
                x == thr, jnp.where(row <= last, dist, NEG_BIG), jnp.where(x > thr, dist, NEG_BIG))
            return carry

        _for_key_rows(n_chunks, mark, 0)

    qh = [qt_ref[u * HEAD_DIM:(u + 1) * HEAD_DIM, :] for u in range(hps)]
    slope = [slope_ref[u, 0:1, :] for u in range(hps)]

    def logits(r0, rows, mx):
        kk = k_ref[pl.ds(r0, rows), :]
        nd = sel_scr[pl.ds(r0, rows), :]
        out = []
        for u in range(hps):
            s = jnp.dot(kk[:, u * HEAD_DIM:(u + 1) * HEAD_DIM], qh[u], preferred_element_type=F32)
            s = s + nd * slope[u]
            s_scr[u, pl.ds(r0, rows), :] = s
            out.append(jnp.maximum(mx[u], s.reshape(rows // 8, 8, qb).max(axis=0)))
        return tuple(out)

    mx8 = _for_key_rows(n_chunks, logits, tuple(jnp.full((8, qb), -3e38, F32) for _ in range(hps)))
    m = [x.max(axis=0, keepdims=True) for x in mx8]

    def weigh(r0, rows, carry):
        c0 = r0 // sc
        out = []
        for u in range(hps):
            l8, acc = carry[u]
            p = jnp.exp2(s_scr[u, pl.ds(r0, rows), :] - m[u])
            l8 = l8 + p.reshape(rows // 8, 8, qb).sum(axis=0)
            v = jnp.concatenate([vt_ref[u, c0 + j] for j in range(rows // sc)], axis=1)
            acc = acc + jnp.dot(v, p.astype(BF16), preferred_element_type=F32)
            out.append((l8, acc))
        return tuple(out)

    zero = (jnp.zeros((8, qb), F32), jnp.zeros((HEAD_DIM, qb), F32))
    res = _for_key_rows(n_chunks, weigh, tuple(zero for _ in range(hps)))
    for u in range(hps):
        l8, acc = res[u]
        o_ref[u * HEAD_DIM:(u + 1) * HEAD_DIM, :] = acc / l8.sum(axis=0, keepdims=True)


def _prompt_attention(qit, wt, ki, qt, kv_bf, vt4, slopes, t_pad):
    n_blk = t_pad // Q_BLOCK
    hps = HEADS_PER_STEP
    assert t_pad <= TIE_ROW_LIMIT and t_pad % KEY_CHUNKS[-1] == 0
    return pl.pallas_call(
        _prompt_attn_kernel,
        grid=(n_blk, N_HEADS // hps),
        in_specs=[pl.BlockSpec((N_IDX_HEADS * IDX_DIM, Q_BLOCK), lambda i, h: (0, i)),
                  pl.BlockSpec((N_IDX_HEADS, Q_BLOCK), lambda i, h: (0, i)),
                  pl.BlockSpec((t_pad, IDX_DIM), lambda i, h: (0, 0)),
                  pl.BlockSpec((hps * HEAD_DIM, Q_BLOCK), lambda i, h: (h, i)),
                  pl.BlockSpec((t_pad, hps * HEAD_DIM), lambda i, h: (0, h)),
                  pl.BlockSpec((hps, t_pad // S_CHUNK, HEAD_DIM, S_CHUNK), lambda i, h: (h, 0, 0, 0)),
                  pl.BlockSpec((hps, 8, Q_BLOCK), lambda i, h: (h, 0, 0))],
        out_specs=pl.BlockSpec((hps * HEAD_DIM, Q_BLOCK), lambda i, h: (h, i)),
        out_shape=jax.ShapeDtypeStruct((N_HEADS * HEAD_DIM, t_pad), F32),
        scratch_shapes=[pltpu.VMEM((t_pad, Q_BLOCK), F32), pltpu.VMEM((hps, t_pad, Q_BLOCK), F32),
                        pltpu.VMEM((8, Q_BLOCK), I32)],
        compiler_params=_cparams("arbitrary", "arbitrary"),
        name="prompt_attention",
    )(qit, wt, ki, qt, kv_bf, vt4, slopes)


N_SROWS = 4 * N_HEADS
PAGE_ROWS = PAGE * N_HEADS
PAGES_PER_STEP = 8
PAGES_PER_GROUP = 4


def _sample_score_kernel(pt_ref, *refs, n_pages):
    kidx_refs = refs[:n_pages]
    kinew_ref, qi_ref, w_ref, o_ref = refs[n_pages:]
    qi = qi_ref[...]
    w = w_ref[...]

    def page_scores(page):
        s = jnp.dot(qi, page.astype(BF16), preferred_element_type=F32)
        x = jnp.maximum(s, 0.0) * w
        x8 = x.reshape(N_IDX_HEADS // 2, 8, PAGE).sum(axis=0)
        return x8 + pltpu.roll(x8, 4, 0)

    for p in range(n_pages):
        o_ref[:, p * PAGE:(p + 1) * PAGE] = page_scores(kidx_refs[p][...])
    slot = lax.broadcasted_iota(I32, (8, PAGE), 1)
    q = lax.broadcasted_iota(I32, (8, PAGE), 0) % 4
    o_ref[:, n_pages * PAGE:] = jnp.where(slot <= q, page_scores(kinew_ref[...]), NEG_BIG)


def _sample_scores(page_table, cache_kidx_t, kinew_t, qi_s, w_s):
    b, n_pages = page_table.shape

    def page_index(s, pt, *, p):
        return (0, pt[s, p], 0, 0)

    grid_spec = pltpu.PrefetchScalarGridSpec(
        num_scalar_prefetch=1,
        grid=(b,),
        in_specs=[pl.BlockSpec((None, None, IDX_DIM, PAGE), functools.partial(page_index, p=p))
                  for p in range(n_pages)]
        + [pl.BlockSpec((None, IDX_DIM, PAGE), lambda s, pt: (s, 0, 0)),
           pl.BlockSpec((None, N_SROWS, IDX_DIM), lambda s, pt: (s, 0, 0)),
           pl.BlockSpec((None, N_SROWS, PAGE), lambda s, pt: (s, 0, 0))],
        out_specs=pl.BlockSpec((None, 8, (n_pages + 1) * PAGE), lambda s, pt: (s, 0, 0)),
    )
    return pl.pallas_call(
        functools.partial(_sample_score_kernel, n_pages=n_pages),
        grid_spec=grid_spec,
        out_shape=jax.ShapeDtypeStruct((b, 8, (n_pages + 1) * PAGE), F32),
        compiler_params=_cparams("arbitrary"),
        name="sample_scores",
    )(page_table, *([cache_kidx_t] * n_pages), kinew_t, qi_s, w_s)


def _sample_select_kernel(s_ref, e_ref, o_ref, tie_scr):
    sc = s_ref[...]
    rows, width = sc.shape

    def count_ge(cand):
        return jnp.sum(jnp.where(sc >= cand, 1.0, 0.0), axis=1, keepdims=True)

    thr, n_ge = _bisect_threshold(count_ge, jnp.zeros((rows, 1), I32))
    pos = lax.broadcasted_iota(I32, (rows, width), 1)
    tie_scr[...] = jnp.full(tie_scr.shape, TIE_ROW_LIMIT, I32)

    @pl.when(jnp.max(n_ge) > TOPK)
    def _():
        need = TOPK - jnp.sum(jnp.where(sc > thr, 1.0, 0.0), axis=1, keepdims=True)

        def tie_body(p, last):
            cand = last + jnp.left_shift(jnp.int32(1), TIE_ROW_BITS - 1 - p)
            below = jnp.sum(jnp.where(sc == thr, jnp.where(pos < cand, 1.0, 0.0), 0.0), axis=1, keepdims=True)
            return jnp.where(below < need, cand, last)

        tie_scr[...] = lax.fori_loop(0, TIE_ROW_BITS, tie_body, jnp.zeros((rows, 1), I32))

    last = tie_scr[...]
    sel = jnp.where(sc == thr, jnp.where(pos <= last, 1.0, 0.0), jnp.where(sc > thr, 1.0, 0.0)).astype(BF16)
    n_tiles = width // PAGE
    for t in range(n_tiles):
        x = jnp.dot(sel[:, t * PAGE:(t + 1) * PAGE], e_ref[...], preferred_element_type=F32)
        lo = t * PAGE_ROWS
        hi = min(lo + PAGE_ROWS, o_ref.shape[1])
        o_ref[:, lo:hi] = x[:, :hi - lo]


def _sample_select(scores, expand, out_width):
    rows, width = scores.shape
    rb = 128
    return pl.pallas_call(
        _sample_select_kernel,
        grid=(rows // rb,),
        in_specs=[pl.BlockSpec((rb, width), lambda i: (i, 0)),
                  pl.BlockSpec((PAGE, PAGE_ROWS), lambda i: (0, 0))],
        out_specs=pl.BlockSpec((rb, out_width), lambda i: (i, 0)),
        out_shape=jax.ShapeDtypeStruct((rows, out_width), F32),
        scratch_shapes=[pltpu.VMEM((rb, 1), I32)],
        compiler_params=_cparams("parallel"),
        name="sample_select",
    )(scores, expand)


def _sample_attn_kernel(pt_ref, q_ref, *refs, n_steps, past_len):
    pps = PAGES_PER_STEP
    k_refs, v_refs = refs[:pps], refs[pps:2 * pps]
    kn_ref, vn_ref, selp_ref, seln_ref, slope_ref, hm_ref, o_ref, m_scr, l_scr, acc_scr = refs[2 * pps:]
    jj = pl.program_id(1)

    @pl.when(jj == 0)
    def _():
        m_scr[...] = jnp.full(m_scr.shape, -3e38, F32)
        l_scr[...] = jnp.zeros(l_scr.shape, F32)
        acc_scr[...] = jnp.zeros(acc_scr.shape, F32)

    def step(k_list, v_list, sel8, pos0):
        q = q_ref[...]
        s = jnp.concatenate(
            [lax.dot_general(q, k.astype(BF16), (((1,), (1,)), ((), ())), preferred_element_type=F32)
             for k in k_list], axis=1)
        width = s.shape[1]
        lane = lax.broadcasted_iota(I32, (8, width), 1)
        rq = lax.broadcasted_iota(I32, (8, width), 0) % 4
        dist = (pos0 - past_len + lane // N_HEADS - rq).astype(F32)
        hm = jnp.concatenate([hm_ref[...]] * (width // PAGE_ROWS), axis=1) if width > PAGE_ROWS \
            else hm_ref[:, :width]
        g = N_SROWS // 8
        x = s.reshape(g, 8, width) + slope_ref[...].reshape(g, 8, 1) * dist[None]
        x = jnp.where(sel8[None] > 0.5, x, NEG_BIG) + hm.reshape(g, 8, width)
        x = x.reshape(N_SROWS, width)
        m = m_scr[...]
        m_new = jnp.maximum(m, jnp.max(x, axis=1, keepdims=True))
        alpha = jnp.exp2(m - m_new)
        p = jnp.exp2(x - m_new)
        l_scr[...] = alpha * l_scr[...] + jnp.sum(p, axis=1, keepdims=True)
        pb = p.astype(BF16)
        pv = None
        off = 0
        for v in v_list:
            part = jnp.dot(pb[:, off:off + v.shape[0]], v.astype(BF16), preferred_element_type=F32)
            pv = part if pv is None else pv + part
            off += v.shape[0]
        acc_scr[...] = alpha * acc_scr[...] + pv
        m_scr[...] = m_new

    ppg = PAGES_PER_GROUP
    for g in range(pps // ppg):
        lanes = slice(g * ppg * PAGE_ROWS, (g + 1) * ppg * PAGE_ROWS)
        step([r[...] for r in k_refs[g * ppg:(g + 1) * ppg]], [r[...] for r in v_refs[g * ppg:(g + 1) * ppg]],
             selp_ref[:, lanes], (jj * pps + g * ppg) * PAGE)

    @pl.when(jj == n_steps - 1)
    def _():
        step([kn_ref[...]], [vn_ref[...]], seln_ref[...], past_len)
        o_ref[...] = acc_scr[...] / l_scr[...]


def _sample_attention(page_table, q_s, cache_k2, cache_v2, knew, vnew, sel_exp, slope_rows, head_mask):
    b, n_pages = page_table.shape
    pps = PAGES_PER_STEP
    assert n_pages % pps == 0 and pps % PAGES_PER_GROUP == 0
    n_steps = n_pages // pps
    nw = knew.shape[1]

    def page_index(s, jj, pt, *, u):
        return (pt[s, jj * pps + u], 0)

    page_specs = [pl.BlockSpec((PAGE_ROWS, HEAD_DIM), functools.partial(page_index, u=u)) for u in range(pps)]
    grid_spec = pltpu.PrefetchScalarGridSpec(
        num_scalar_prefetch=1,
        grid=(b, n_steps),
        in_specs=[pl.BlockSpec((None, N_SROWS, HEAD_DIM), lambda s, jj, pt: (s, 0, 0))]
        + page_specs + page_specs
        + [pl.BlockSpec((None, nw, HEAD_DIM), lambda s, jj, pt: (s, 0, 0)),
           pl.BlockSpec((None, nw, HEAD_DIM), lambda s, jj, pt: (s, 0, 0)),
           pl.BlockSpec((None, 8, pps * PAGE_ROWS), lambda s, jj, pt: (s, 0, jj)),
           pl.BlockSpec((None, 8, nw), lambda s, jj, pt: (s, 0, n_pages * PAGE_ROWS // nw)),
           pl.BlockSpec((N_SROWS, 1), lambda s, jj, pt: (0, 0)),
           pl.BlockSpec((N_SROWS, PAGE_ROWS), lambda s, jj, pt: (0, 0))],
        out_specs=pl.BlockSpec((None, N_SROWS, HEAD_DIM), lambda s, jj, pt: (s, 0, 0)),
        scratch_shapes=[pltpu.VMEM((N_SROWS, 1), F32), pltpu.VMEM((N_SROWS, 1), F32),
                        pltpu.VMEM((N_SROWS, HEAD_DIM), F32)],
    )
    return pl.pallas_call(
        functools.partial(_sample_attn_kernel, n_steps=n_steps, past_len=n_pages * PAGE),
        grid_spec=grid_spec,
        out_shape=jax.ShapeDtypeStruct((b, N_SROWS, HEAD_DIM), F32),
        compiler_params=_cparams("arbitrary", "arbitrary"),
        name="sample_attention",
    )(page_table, q_s, *([cache_k2] * pps), *([cache_v2] * pps), knew, vnew, sel_exp, sel_exp, slope_rows,
      head_mask)


def _merge_out_kernel(ot_ref, os_ref, ga_ref, y_ref, wa_ref, wb_ref, wo_ref, g1_ref, g2_ref, x_ref, fg_ref,
                      out_ref, *, n_prompt_blocks):
    o = jnp.where(pl.program_id(0) < n_prompt_blocks, ot_ref[...].T, os_ref[...])
    ga = ga_ref[...]
    og = (o * (ga * _sigmoid(ga))).astype(BF16)
    pa = jnp.dot(y_ref[...], wa_ref[...], preferred_element_type=F32)
    pb = jnp.dot(og, wb_ref[...], preferred_element_type=F32)
    merged = (_sigmoid(g1_ref[...]) * pa + _sigmoid(g2_ref[...]) * pb).astype(BF16)
    res = x_ref[...] + jnp.dot(merged, wo_ref[...], preferred_element_type=F32)
    y = res * lax.rsqrt(jnp.mean(res * res, axis=-1, keepdims=True) + NORM_EPS)
    out_ref[...] = y * fg_ref[...]


def _merge_out(ot, o_s, ga, y_rnn, w_pa, w_pb, w_out, gg, x, fg):
    m, d = ga.shape
    npb = ot.shape[1] // ROW_BLOCK
    row = lambda i: (i, 0)
    weight = pl.BlockSpec((d, d), lambda i: (0, 0), pipeline_mode=pl.Buffered(1))
    return pl.pallas_call(
        functools.partial(_merge_out_kernel, n_prompt_blocks=npb),
        grid=(m // ROW_BLOCK,),
        in_specs=[pl.BlockSpec((d, ROW_BLOCK), lambda i: (0, jnp.minimum(i, npb - 1))),
                  pl.BlockSpec((ROW_BLOCK, d), lambda i: (jnp.maximum(i - npb, 0), 0)),
                  pl.BlockSpec((ROW_BLOCK, d), row), pl.BlockSpec((ROW_BLOCK, d), row),
                  weight, weight, weight,
                  pl.BlockSpec((ROW_BLOCK, d), row), pl.BlockSpec((ROW_BLOCK, d), lambda i: (i, 1)),
                  pl.BlockSpec((ROW_BLOCK, d), row), pl.BlockSpec((1, d), lambda i: (0, 0))],
        out_specs=pl.BlockSpec((ROW_BLOCK, d), row),
        out_shape=jax.ShapeDtypeStruct((m, d), F32),
        compiler_params=pltpu.CompilerParams(dimension_semantics=("parallel",),
                                             vmem_limit_bytes=MERGE_VMEM_LIMIT),
        name="merge_out",
    )(ot, o_s, ga, y_rnn, w_pa, w_pb, w_out, gg, gg, x, fg.reshape(1, d))


def kernel(x_prompt, x_sample, cache_k, cache_v, cache_kidx, state_h, state_conv, page_table, meta_tokens,
           norm_g, w_in, conv_w, conv_b, lru_wa, lru_ba, lru_wx, lru_bx, lru_lambda, w_proj_a, w_proj_b,
           w_out, final_g):
    assert x_prompt.shape[0] == 1 and norm_g.shape[0] == 1
    d = D_MODEL
    seq = x_prompt.shape[1]
    t_p = seq + N_META
    t_pad = -(-t_p // Q_BLOCK) * Q_BLOCK
    n_seq, n_new = x_sample.shape[:2]
    n_s = n_seq * n_new
    n_pages = page_table.shape[1]
    past_len = n_pages * PAGE
    assert n_new == 4 and n_s % ROW_BLOCK == 0
    m_all = t_pad + n_s

    x_all = jnp.concatenate([meta_tokens.astype(F32), x_prompt[0], jnp.zeros((t_pad - t_p, d), F32),
                             x_sample.reshape(n_s, d)], axis=0)
    xn = _rmsnorm_bf16(x_all, norm_g[0])

    wb = w_in[0].astype(BF16)
    cuts = np.cumsum([d, d, d, d, d, d, N_IDX_HEADS * IDX_DIM, IDX_DIM, N_IDX_HEADS, d, d]).tolist()
    rg = _matmul(xn, wb, F32, cols=(0, cuts[1]), name="proj_rnn")
    qv = _matmul(xn, wb, BF16, cols=(cuts[1], cuts[2]), scale=LOG2E * HEAD_DIM ** -0.5, name="proj_q")
    kv, kv_bf = _matmul(xn, wb, (F32, BF16), cols=(cuts[2], cuts[4]), name="proj_kv")
    ga = _matmul(xn, wb, F32, cols=(cuts[4], cuts[5]), name="proj_ga")
    qi = _matmul(xn, wb, BF16, cols=(cuts[5], cuts[6]), scale=IDX_DIM ** -0.5, name="proj_qi")
    w_kw = jnp.pad(wb[:, cuts[6]:cuts[8]], ((0, 0), (0, 128 - (cuts[8] - cuts[6]))))
    kw = _matmul(xn, w_kw, F32, name="proj_kiwi")
    gg = _matmul(xn, wb[:, cuts[8]:], F32, name="proj_gates")

    xr, gr = rg[:, :d], rg[:, d:]
    k_all, v_all = kv[:, :d], kv[:, d:]
    ki_all = kw[:, :IDX_DIM]
    wi_all = kw[:, IDX_DIM:IDX_DIM + N_IDX_HEADS] * (N_IDX_HEADS ** -0.5)

    cw, cb = conv_w[0], conv_b[0].reshape(1, d)
    wa, wx = lru_wa[0].astype(BF16), lru_wx[0].astype(BF16)
    ba, bx, lam = lru_ba[0].reshape(1, d), lru_bx[0].reshape(1, d), lru_lambda[0].reshape(1, d)
    hs_p, y_p = _lru_prompt(rg, cw, cb, wa, wx, ba, bx, lam, t_pad)
    xr_s = jnp.swapaxes(xr[t_pad:].reshape(n_seq, n_new, d), 0, 1)
    gr_s = jnp.swapaxes(gr[t_pad:].reshape(n_seq, n_new, d), 0, 1)
    xs = jnp.concatenate([jnp.swapaxes(state_conv[0], 0, 1), xr_s], axis=0)
    hs_s, y_s = _lru_sample(xs, gr_s, state_h[0], cw, cb, wa, wx, ba, bx, lam)
    y_rnn = jnp.concatenate([y_p, jnp.swapaxes(y_s, 0, 1).reshape(n_s, d)], axis=0)

    slopes = jnp.asarray(LOG2E * 2.0 ** (-8.0 * np.arange(1, N_HEADS + 1) / N_HEADS), dtype=F32)
    qit = qi.T
    wt = wi_all.T
    qt = qv.T
    vt4 = kv_bf[:t_pad, d:].reshape(t_pad // S_CHUNK, S_CHUNK, N_HEADS, HEAD_DIM).transpose(2, 0, 3, 1)
    slope_b = jnp.broadcast_to(slopes[:, None, None], (N_HEADS, 8, Q_BLOCK))
    ot = _prompt_attention(qit, wt, ki_all.astype(BF16), qt, kv_bf, vt4, slope_b, t_pad)

    hq = lambda a, width: a.reshape(n_seq, n_new, N_HEADS, width).transpose(0, 2, 1, 3).reshape(
        n_seq, N_HEADS * n_new, width)
    qi_s = hq(qi[t_pad:], IDX_DIM)
    w_s = jnp.broadcast_to(
        wi_all[t_pad:].reshape(n_seq, n_new, N_IDX_HEADS).transpose(0, 2, 1).reshape(n_seq, -1, 1),
        (n_seq, N_IDX_HEADS * n_new, PAGE))
    kinew_t = jnp.pad(jnp.swapaxes(ki_all[t_pad:].reshape(n_seq, n_new, IDX_DIM), 1, 2),
                      ((0, 0), (0, 0), (0, PAGE - n_new)))
    scores = _sample_scores(page_table, jnp.swapaxes(cache_kidx, 2, 3), kinew_t, qi_s, w_s)
    col = np.arange(PAGE_ROWS)
    expand = jnp.asarray(col[None, :] // N_HEADS == np.arange(PAGE)[:, None], dtype=BF16)
    head_mask = jnp.asarray(np.where(np.arange(N_SROWS)[:, None] // n_new == col[None, :] % N_HEADS,
                                     0.0, 4.0 * NEG_BIG), dtype=F32)
    sel_exp = _sample_select(scores.reshape(n_seq * 8, past_len + PAGE), expand, (past_len + 8) * N_HEADS)
    sel_exp = sel_exp.reshape(n_seq, 8, (past_len + 8) * N_HEADS)
    new_rows = lambda a: jnp.pad(a.reshape(n_seq, n_new * N_HEADS, HEAD_DIM),
                                 ((0, 0), (0, (8 - n_new) * N_HEADS), (0, 0)))
    slope_rows = jnp.repeat(slopes, n_new).reshape(N_SROWS, 1)
    o_s = _sample_attention(page_table, hq(qv[t_pad:], HEAD_DIM),
                            cache_k.reshape(-1, HEAD_DIM), cache_v.reshape(-1, HEAD_DIM),
                            new_rows(k_all[t_pad:]), new_rows(v_all[t_pad:]), sel_exp, slope_rows, head_mask)
    o_s = o_s.reshape(n_seq, N_HEADS, n_new, HEAD_DIM).transpose(0, 2, 1, 3).reshape(n_s, d)

    y_all = _merge_out(ot, o_s, ga, y_rnn, w_proj_a[0].astype(BF16), w_proj_b[0].astype(BF16),
                       w_out[0].astype(BF16), gg, x_all, final_g)

    y_prompt = y_all[N_META:t_p][None]
    y_sample = y_all[t_pad:].reshape(n_seq, n_new, d)
    heads = lambda a, *lead: a.reshape(*lead, N_HEADS, HEAD_DIM)
    k_prompt = heads(k_all[:t_p], 1, 1, t_p)
    v_prompt = heads(v_all[:t_p], 1, 1, t_p)
    kidx_prompt = ki_all[:t_p][None, None]
    h_prompt = hs_p[t_p - 1][None, None]
    conv_prompt = xr[t_p - (CONV_W - 1):t_p][None, None]
    k_sample = heads(k_all[t_pad:], 1, n_seq, n_new)
    v_sample = heads(v_all[t_pad:], 1, n_seq, n_new)
    kidx_sample = ki_all[t_pad:].reshape(1, n_seq, n_new, IDX_DIM)
    h_sample = hs_s[n_new - 1][None]
    conv_sample = jnp.swapaxes(xs[n_new:], 0, 1)[None]
    return (y_prompt, y_sample, k_prompt, v_prompt, kidx_prompt, h_prompt, conv_prompt,
            k_sample, v_sample, kidx_sample, h_sample, conv_sample)
```
